```python
import jax, jax.numpy as jnp
from jax import lax
import numpy as np

D_MODEL = 2048
BATCH = 2
SEQ = 4096
DEPTH = 2
DEC_BATCH = 8
DEC_SEQ = 4096
PAST_LEN = 128

D_FF = 5632
NORM_EPS = 1e-6

ATT_GROUPS = ((128, 1), (512, 4), (2048, 16))
N_ATT_GROUPS = 3
ATT_HEADS = 4
ATT_HEAD_DIM = 128
ATT_OUT_WIDTH = ATT_HEADS * ATT_HEAD_DIM
ATT_QKV_WIDTH = N_ATT_GROUPS * ATT_OUT_WIDTH

RET_HEADS = 4
RET_DK = 128
RET_DV = 256
RET_CHUNK = 128
RET_QK_WIDTH = RET_HEADS * RET_DK
RET_V_WIDTH = RET_HEADS * RET_DV
RET_NORM_EPS = 1e-5

RWKV_HEADS = 8
RWKV_N = 64
RWKV_WIDTH = RWKV_HEADS * RWKV_N
RWKV_DECAY_RANK = 64
RWKV_A_RANK = 64
RWKV_GATE_RANK = 128
RWKV_CONV = 3
RWKV_CONV_CH = 3 * RWKV_WIDTH + 2 * RWKV_DECAY_RANK + 2 * RWKV_A_RANK + RWKV_GATE_RANK
RWKV_NORM_EPS = 64e-5

N_BRANCH = 3
IN_SPLITS = (ATT_QKV_WIDTH, ATT_QKV_WIDTH, ATT_QKV_WIDTH, RET_QK_WIDTH, RET_QK_WIDTH, RET_V_WIDTH, RET_V_WIDTH, RWKV_CONV_CH, N_BRANCH * D_MODEL)
D_IN = 15744

kernel_name = 'hybrid_bidir_dilated_retention_rwkv7_encoder'


def _offsets(sizes):
    out, acc = [], 0
    for n in sizes[:-1]:
        acc += n
        out.append(acc)
    return out


def _rms_norm(x, gain):
    x32 = x.astype(jnp.float32)
    y = x32 * lax.rsqrt(jnp.mean(x32 * x32, axis=-1, keepdims=True) + NORM_EPS)
    return (y * gain.astype(jnp.float32)).astype(x.dtype)


def _head_norm(y, gain, eps):
    mu = jnp.mean(y, axis=-1, keepdims=True)
    var = jnp.mean(jnp.square(y - mu), axis=-1, keepdims=True)
    return (y - mu) * lax.rsqrt(var + eps) * gain.astype(jnp.float32)


def _swiglu(x, w_gate, w_up, w_down):
    return (jax.nn.silu(x @ w_gate) * (x @ w_up)) @ w_down


def _alibi_slopes():
    n = N_ATT_GROUPS * ATT_HEADS
    return jnp.asarray(np.power(np.float32(2.0), -8.0 * np.arange(1, n + 1, dtype=np.float32) / n), jnp.float32)


def _dilated_group_attention(q, k, v, window, dilation, slopes):
    b, s, h, dh = q.shape
    half = window // (2 * dilation)
    blk = half
    sub_len = s // dilation
    nb = -(-sub_len // blk)
    lp = nb * blk

    def to_phase(t):
        t = t.reshape(b, sub_len, dilation, h, dh).transpose(0, 2, 1, 3, 4)
        t = jnp.pad(t, ((0, 0), (0, 0), (0, lp - sub_len), (0, 0), (0, 0)))
        return t.reshape(b, dilation, nb, blk, h, dh)

    def band(t):
        tp = jnp.pad(t, ((0, 0), (0, 0), (1, 1), (0, 0), (0, 0), (0, 0)))
        return jnp.concatenate([tp[:, :, :-2], tp[:, :, 1:-1], tp[:, :, 2:]], axis=3)

    qb = to_phase(q)
    kband = band(to_phase(k))
    vband = band(to_phase(v))
    scores = jnp.einsum('bpnqhd,bpnkhd->bpnhqk', qb, kband, preferred_element_type=jnp.float32) * (dh ** -0.5)
    qi = jnp.arange(blk)[:, None]
    kj = jnp.arange(3 * blk)[None, :]
    rel = kj - blk - qi
    key_idx = jnp.arange(nb)[:, None, None] * blk + (kj - blk)[None]
    valid = (jnp.abs(rel)[None] <= half) & (key_idx >= 0) & (key_idx < sub_len)
    dist = (dilation * jnp.abs(rel)).astype(jnp.float32)
    bias = -slopes[:, None, None] * dist[None]
    scores = jnp.where(valid[None, None, :, None], scores + bias[None, None, None], -jnp.inf)
    lse = jax.nn.logsumexp(scores, axis=-1)
    probs = jnp.exp(scores - lse[..., None])
    out = jnp.einsum('bpnhqk,bpnkhd->bpnqhd', probs.astype(v.dtype), vband, preferred_element_type=jnp.float32)

    def from_phase(t):
        t = t.reshape((b, dilation, lp) + t.shape[4:])[:, :, :sub_len]
        t = jnp.swapaxes(t, 1, 2)
        return t.reshape((b, s) + t.shape[3:])

    return from_phase(out), from_phase(jnp.swapaxes(lse, 3, 4))


def _dilated_attention(q, k, v):
    slopes = _alibi_slopes()
    outs, lses = [], []
    for g, (window, dilation) in enumerate(ATT_GROUPS):
        o, l = _dilated_group_attention(q[:, :, g], k[:, :, g], v[:, :, g], window, dilation, slopes[g * ATT_HEADS:(g + 1) * ATT_HEADS])
        outs.append(o)
        lses.append(l)
    weights = jax.nn.softmax(jnp.stack(lses, axis=0), axis=0)
    return jnp.einsum('gbsh,gbshd->bshd', weights, jnp.stack(outs, axis=0))


def _retention(q, k, v, decay_logit):
    b, s, h, dk = q.shape
    dv = v.shape[-1]
    c = RET_CHUNK
    nc = s // c
    f32 = jnp.float32

    def both(t):
        t = t.astype(f32)
        return jnp.stack([t, t[:, ::-1]], axis=0).reshape(2, b, nc, c, h, t.shape[-1])

    qd, kd, vd = both(q), both(k * (dk ** -0.5)), both(v)
    log_g = jax.nn.log_sigmoid(decay_logit.astype(f32))
    pos = jnp.arange(c, dtype=f32)
    diff = pos[:, None] - pos[None, :]
    dmat = jnp.where(diff >= 0, jnp.exp(jnp.maximum(diff, 0.0)[None, None] * log_g[:, :, None, None]), 0.0)
    scores = jnp.einsum('zbnihd,zbnjhd->zbnhij', qd, kd) * dmat[:, None, None]
    inner = jnp.einsum('zbnhij,zbnjhe->zbnihe', scores, vd)
    zeta = jnp.exp((c - 1 - pos)[None, :, None] * log_g[:, None, :])
    xi = jnp.exp((pos + 1)[None, :, None] * log_g[:, None, :])
    chunk_kv = jnp.einsum('zbnjhd,zbnjhe,zjh->nzbhde', kd, vd, zeta)
    g_chunk = jnp.exp(c * log_g)[:, None, :, None, None]

    def step(state, kv):
        return g_chunk * state + kv, state

    _, prev = lax.scan(step, jnp.zeros((2, b, h, dk, dv), f32), chunk_kv)
    cross = jnp.einsum('zbnihd,nzbhde->zbnihe', qd, prev) * xi[:, None, None, :, :, None]
    ret = (inner + cross).reshape(2, b, s, h, dv)
    return ret[0] + ret[1][:, ::-1]


def _rwkv7(feats, p):
    b, s, _ = feats.shape
    f32 = jnp.float32
    dtype = feats.dtype
    feats = lax.conv_general_dilated(feats, p['rwkv_conv'][:, None, :].astype(dtype), window_strides=(1,), padding=((RWKV_CONV // 2, RWKV_CONV // 2),), dimension_numbers=('NWC', 'WIO', 'NWC'), feature_group_count=RWKV_CONV_CH)
    r, k, v, w_lo, a_lo, g_lo = jnp.split(feats, _offsets((RWKV_WIDTH, RWKV_WIDTH, RWKV_WIDTH, 2 * RWKV_DECAY_RANK, 2 * RWKV_A_RANK, RWKV_GATE_RANK)), axis=-1)
    w_lo = w_lo.reshape(b, s, 2, RWKV_DECAY_RANK).astype(f32)
    a_lo = a_lo.reshape(b, s, 2, RWKV_A_RANK).astype(f32)
    logw = -jax.nn.softplus(-(p['rwkv_w0'].astype(f32) + jnp.einsum('bszr,zrc->bszc', jnp.tanh(w_lo), p['rwkv_w2'].astype(f32)))) - 0.5
    decay = jnp.exp(-jnp.exp(logw))
    a = jax.nn.sigmoid(p['rwkv_a0'].astype(f32) + jnp.einsum('bszr,zrc->bszc', a_lo, p['rwkv_a2'].astype(f32)))
    g = jax.nn.sigmoid(g_lo.astype(f32)) @ p['rwkv_g2'].astype(f32)

    def heads(t):
        return t.reshape(t.shape[:-1] + (RWKV_HEADS, RWKV_N))

    k32 = k.astype(f32)
    kk = heads(k32 * p['rwkv_k_k'].astype(f32))
    kk = kk * lax.rsqrt(jnp.sum(kk * kk, axis=-1, keepdims=True) + 1e-12)
    k_dir = heads(k32[:, :, None, :] * (1.0 + (a - 1.0) * p['rwkv_k_a'].astype(f32)))
    r_h = heads(r.astype(f32))
    v_h = heads(v.astype(f32))

    def per_dir(t):
        t = jnp.transpose(t, (1, 2, 0, 3, 4))
        return jnp.stack([t[:, 0], t[::-1, 1]], axis=1)

    def shared(t):
        t = jnp.swapaxes(t, 0, 1)
        return jnp.stack([t, t[::-1]], axis=1)

    xs = (per_dir(heads(decay)), per_dir(k_dir), shared(v_h), shared(r_h), shared(kk), per_dir(heads(a)))

    def step(state, inp):
        w_t, k_t, v_t, r_t, kk_t, a_t = inp
        sa = jnp.einsum('zbhvk,zbhk->zbhv', state, -kk_t)
        state = state * w_t[..., None, :] + sa[..., :, None] * (kk_t * a_t)[..., None, :] + v_t[..., :, None] * k_t[..., None, :]
        return state, jnp.einsum('zbhvk,zbhk->zbhv', state, r_t)

    _, ys = lax.scan(step, jnp.zeros((2, b, RWKV_HEADS, RWKV_N, RWKV_N), f32), xs)
    y = jnp.swapaxes(ys[:, 0] + ys[::-1, 1], 0, 1)
    y = _head_norm(y, heads(p['rwkv_ln_w']), RWKV_NORM_EPS) + heads(p['rwkv_ln_b'].astype(f32))
    bonus = jnp.sum(r_h[:, :, None] * k_dir * p['rwkv_r_k'].astype(f32), axis=-1, keepdims=True)
    y = y + jnp.sum(bonus, axis=2) * v_h
    return (y.reshape(b, s, RWKV_WIDTH) * g).astype(dtype)


def _encoder_layer(x, p):
    b, s, _ = x.shape
    f32 = jnp.float32
    x = x + 0.5 * _swiglu(_rms_norm(x, p['ffn1_norm']), p['ffn1_w_gate'], p['ffn1_w_up'], p['ffn1_w_down'])
    u = _rms_norm(x, p['mix_norm'])
    proj = u @ p['w_in']
    aq, ak, av, rq, rk, rv, rg, cfeat, gate_logits = jnp.split(proj, _offsets(IN_SPLITS), axis=-1)
    att_shape = (b, s, N_ATT_GROUPS, ATT_HEADS, ATT_HEAD_DIM)
    y_a = _dilated_attention(aq.reshape(att_shape), ak.reshape(att_shape), av.reshape(att_shape))
    y_a = y_a.reshape(b, s, ATT_OUT_WIDTH).astype(x.dtype)
    ret = _retention(rq.reshape(b, s, RET_HEADS, RET_DK), rk.reshape(b, s, RET_HEADS, RET_DK), rv.reshape(b, s, RET_HEADS, RET_DV), p['ret_decay_logit'])
    y_b = (jax.nn.silu(rg.astype(f32)) * _head_norm(ret, p['ret_norm'], RET_NORM_EPS).reshape(b, s, RET_V_WIDTH)).astype(x.dtype)
    y_c = _rwkv7(cfeat, p)
    gates = jax.nn.sigmoid(gate_logits.reshape(b, s, N_BRANCH, D_MODEL))
    merged = (gates[:, :, 0] * (y_a @ p['w_branch_a']) + gates[:, :, 1] * (y_b @ p['w_branch_b']) + gates[:, :, 2] * (y_c @ p['w_branch_c']))
    x = x + merged @ p['w_out']
    x = x + 0.5 * _swiglu(_rms_norm(x, p['ffn2_norm']), p['ffn2_w_gate'], p['ffn2_w_up'], p['ffn2_w_down'])
    return x


def setup_inputs(seed: int = 0) -> dict:
    key = jax.random.key(seed)
    ks = iter(jax.random.split(key, 40))
    f32 = jnp.float32

    def nrm(shape, scale):
        return scale * jax.random.normal(next(ks), shape, f32)

    def gain(shape):
        return 1.0 + 0.02 * jax.random.normal(next(ks), shape, f32)

    ret_base = jnp.log(jnp.power(2.0, 5.0 + jnp.arange(RET_HEADS, dtype=f32)) - 1.0)
    conv_base = jnp.array([0.25, 0.5, 0.25], f32)[None, :, None]
    return {
        'x_prompt': nrm((BATCH, SEQ, D_MODEL), 1.0),
        'x_sample': nrm((DEC_BATCH, DEC_SEQ, D_MODEL), 1.0),
        'ffn1_norm': gain((DEPTH, D_MODEL)),
        'ffn1_w_gate': nrm((DEPTH, D_MODEL, D_FF), D_MODEL ** -0.5),
        'ffn1_w_up': nrm((DEPTH, D_MODEL, D_FF), D_MODEL ** -0.5),
        'ffn1_w_down': nrm((DEPTH, D_FF, D_MODEL), D_FF ** -0.5),
        'mix_norm': gain((DEPTH, D_MODEL)),
        'w_in': nrm((DEPTH, D_MODEL, D_IN), D_MODEL ** -0.5),
        'ret_decay_logit': ret_base[None, None, :] + nrm((DEPTH, 2, RET_HEADS), 0.1),
        'ret_norm': gain((DEPTH, RET_HEADS, RET_DV)),
        'rwkv_conv': conv_base + nrm((DEPTH, RWKV_CONV, RWKV_CONV_CH), 0.05),
        'rwkv_w0': jax.random.uniform(next(ks), (DEPTH, 2, RWKV_WIDTH), f32, -6.0, -1.0),
        'rwkv_w2': nrm((DEPTH, 2, RWKV_DECAY_RANK, RWKV_WIDTH), 0.1 * RWKV_DECAY_RANK ** -0.5),
        'rwkv_a0': nrm((DEPTH, 2, RWKV_WIDTH), 0.1),
        'rwkv_a2': nrm((DEPTH, 2, RWKV_A_RANK, RWKV_WIDTH), 0.1 * RWKV_A_RANK ** -0.5),
        'rwkv_g2': nrm((DEPTH, RWKV_GATE_RANK, RWKV_WIDTH), RWKV_GATE_RANK ** -0.5),
        'rwkv_k_k': 0.85 + nrm((DEPTH, RWKV_WIDTH), 0.05),
        'rwkv_k_a': 1.0 + nrm((DEPTH, RWKV_WIDTH), 0.05),
        'rwkv_r_k': nrm((DEPTH, RWKV_HEADS, RWKV_N), 0.1),
        'rwkv_ln_w': gain((DEPTH, RWKV_WIDTH)),
        'rwkv_ln_b': nrm((DEPTH, RWKV_WIDTH), 0.02),
        'w_branch_a': nrm((DEPTH, ATT_OUT_WIDTH, D_MODEL), ATT_OUT_WIDTH ** -0.5),
        'w_branch_b': nrm((DEPTH, RET_V_WIDTH, D_MODEL), RET_V_WIDTH ** -0.5),
        'w_branch_c': nrm((DEPTH, RWKV_WIDTH, D_MODEL), RWKV_WIDTH ** -0.5),
        'w_out': nrm((DEPTH, D_MODEL, D_MODEL), D_MODEL ** -0.5),
        'ffn2_norm': gain((DEPTH, D_MODEL)),
        'ffn2_w_gate': nrm((DEPTH, D_MODEL, D_FF), D_MODEL ** -0.5),
        'ffn2_w_up': nrm((DEPTH, D_MODEL, D_FF), D_MODEL ** -0.5),
        'ffn2_w_down': nrm((DEPTH, D_FF, D_MODEL), D_FF ** -0.5),
        'final_norm': gain((D_MODEL,)),
    }


def reference(x_prompt, x_sample, ffn1_norm, ffn1_w_gate, ffn1_w_up, ffn1_w_down, mix_norm, w_in, ret_decay_logit, ret_norm, rwkv_conv, rwkv_w0, rwkv_w2, rwkv_a0, rwkv_a2, rwkv_g2, rwkv_k_k, rwkv_k_a, rwkv_r_k, rwkv_ln_w, rwkv_ln_b, w_branch_a, w_branch_b, w_branch_c, w_out, ffn2_norm, ffn2_w_gate, ffn2_w_up, ffn2_w_down, final_norm):
    def run(x):
        for l in range(DEPTH):
            p = {
                'ffn1_norm': ffn1_norm[l], 'ffn1_w_gate': ffn1_w_gate[l], 'ffn1_w_up': ffn1_w_up[l], 'ffn1_w_down': ffn1_w_down[l],
                'mix_norm': mix_norm[l], 'w_in': w_in[l],
                'ret_decay_logit': ret_decay_logit[l], 'ret_norm': ret_norm[l],
                'rwkv_conv': rwkv_conv[l], 'rwkv_w0': rwkv_w0[l], 'rwkv_w2': rwkv_w2[l], 'rwkv_a0': rwkv_a0[l], 'rwkv_a2': rwkv_a2[l],
                'rwkv_g2': rwkv_g2[l], 'rwkv_k_k': rwkv_k_k[l], 'rwkv_k_a': rwkv_k_a[l], 'rwkv_r_k': rwkv_r_k[l],
                'rwkv_ln_w': rwkv_ln_w[l], 'rwkv_ln_b': rwkv_ln_b[l],
                'w_branch_a': w_branch_a[l], 'w_branch_b': w_branch_b[l], 'w_branch_c': w_branch_c[l], 'w_out': w_out[l],
                'ffn2_norm': ffn2_norm[l], 'ffn2_w_gate': ffn2_w_gate[l], 'ffn2_w_up': ffn2_w_up[l], 'ffn2_w_down': ffn2_w_down[l],
            }
            x = _encoder_layer(x, p)
        return _rms_norm(x, final_norm)

    y_prompt = run(x_prompt)
    y_sample = run(x_sample)
    return (y_prompt, y_sample)
```

```python
import functools

import numpy as np
import jax
import jax.numpy as jnp
from jax import lax
from jax.experimental import pallas as pl
from jax.experimental.pallas import tpu as pltpu

F32 = jnp.float32
BF16 = jnp.bfloat16

D_MODEL = 2048
D_FF = 5632
DEPTH = 2
NORM_EPS = 1e-6

ATT_GROUPS = ((128, 1), (512, 4), (2048, 16))
ATT_HEADS = 4
ATT_HEAD_DIM = 128
ATT_OUT_WIDTH = ATT_HEADS * ATT_HEAD_DIM
ATT_HALF = 64
ATT_TQ = 128

RET_HEADS = 4
RET_DK = 128
RET_DV = 256
RET_CHUNK = 128
RET_NORM_EPS = 1e-5

RWKV_HEADS = 8
RWKV_N = 64
RWKV_WIDTH = RWKV_HEADS * RWKV_N
RWKV_CONV_CH = 1920
RWKV_NORM_EPS = 64e-5
RWKV_CHUNK = 64

ROW_TILE = 512
VMEM_LIMIT_BYTES = 56 * 1024 * 1024
NEG_BIG = -1e30


def _params(*sem):
    return pltpu.CompilerParams(dimension_semantics=sem, vmem_limit_bytes=VMEM_LIMIT_BYTES)


def _dot(a, b):
    return jnp.dot(a, b, preferred_element_type=F32)


def _dot_nt(a, b):
    return lax.dot_general(a, b, (((1,), (1,)), ((), ())), preferred_element_type=F32)


def _dot_tn(a, b):
    return lax.dot_general(a, b, (((0,), (0,)), ((), ())), preferred_element_type=F32)


def _split2(a):
    hi = a.astype(BF16)
    lo = (a - hi.astype(F32)).astype(BF16)
    return hi, lo


def _rms(x, gain):
    ms = jnp.mean(x * x, axis=-1, keepdims=True)
    return x * lax.rsqrt(ms + NORM_EPS) * gain


def _sigmoid(x):
    return 1.0 / (1.0 + jnp.exp(-x))


def _softplus(x):
    return jnp.maximum(x, 0.0) + jnp.log(1.0 + jnp.exp(-jnp.abs(x)))


def _ffn_kernel(x_ref, g_ref, wg_ref, wu_ref, wd_ref, fg_ref, o_ref, xn_ref, *, nf, final):
    f = pl.program_id(1)

    @pl.when(f == 0)
    def _():
        xn_ref[...] = _rms(x_ref[...], g_ref[...]).astype(BF16)
        o_ref[...] = jnp.zeros_like(o_ref)

    xn = xn_ref[...]
    hg = _dot(xn, wg_ref[...])
    hu = _dot(xn, wu_ref[...])
    h = (hg * _sigmoid(hg) * hu).astype(BF16)
    o_ref[...] += _dot(h, wd_ref[...])

    @pl.when(f == nf - 1)
    def _():
        y = x_ref[...] + 0.5 * o_ref[...]
        if final:
            y = _rms(y, fg_ref[...])
        o_ref[...] = y


def _ffn(x, gain, wg, wu, wd, final_gain, final):
    m, d = x.shape
    ff = wg.shape[1]
    tm, tf = ROW_TILE, 512
    nf = ff // tf
    return pl.pallas_call(
        functools.partial(_ffn_kernel, nf=nf, final=final),
        grid=(m // tm, nf),
        in_specs=[
            pl.BlockSpec((tm, d), lambda i, f: (i, 0)),
            pl.BlockSpec((1, d), lambda i, f: (0, 0)),
            pl.BlockSpec((d, tf), lambda i, f: (0, f)),
            pl.BlockSpec((d, tf), lambda i, f: (0, f)),
            pl.BlockSpec((tf, d), lambda i, f: (f, 0)),
            pl.BlockSpec((1, d), lambda i, f: (0, 0)),
        ],
        out_specs=pl.BlockSpec((tm, d), lambda i, f: (i, 0)),
        out_shape=jax.ShapeDtypeStruct((m, d), F32),
        scratch_shapes=[pltpu.VMEM((tm, d), BF16)],
        compiler_params=_params("parallel", "arbitrary"),
        name="ffn",
    )(x, gain.reshape(1, d), wg, wu, wd, final_gain.reshape(1, d))


def _norm_matmul_kernel(x_ref, g_ref, w_ref, o_ref, xn_ref):
    @pl.when(pl.program_id(1) == 0)
    def _():
        xn_ref[...] = _rms(x_ref[...], g_ref[...]).astype(BF16)

    o_ref[...] = _dot(xn_ref[...], w_ref[...])


def _norm_matmul(x, gain, w, tn):
    m, d = x.shape
    n = w.shape[1]
    tm = ROW_TILE
    return pl.pallas_call(
        _norm_matmul_kernel,
        grid=(m // tm, n // tn),
        in_specs=[
            pl.BlockSpec((tm, d), lambda i, j: (i, 0)),
            pl.BlockSpec((1, d), lambda i, j: (0, 0)),
            pl.BlockSpec((d, tn), lambda i, j: (0, j)),
        ],
        out_specs=pl.BlockSpec((tm, tn), lambda i, j: (i, j)),
        out_shape=jax.ShapeDtypeStruct((m, n), F32),
        scratch_shapes=[pltpu.VMEM((tm, d), BF16)],
        compiler_params=_params("parallel", "arbitrary"),
        name="norm_proj",
    )(x, gain.reshape(1, d), w)


def _attn_kernel(q_ref, kp_ref, kc_ref, kn_ref, vp_ref, vc_ref, vn_ref, o_ref, l_ref, *, dil, slopes, sub_len):
    tq, half, dh = ATT_TQ, ATT_HALF, ATT_HEAD_DIM
    i = pl.program_id(2)
    qi = lax.broadcasted_iota(jnp.int32, (tq, tq + 2 * half), 0)
    kj = lax.broadcasted_iota(jnp.int32, (tq, tq + 2 * half), 1)
    rel = kj - half - qi
    kidx = i * tq - half + kj
    ok = jnp.where(jnp.abs(rel) <= half, 1, 0) * jnp.where(kidx >= 0, 1, 0) * jnp.where(kidx < sub_len, 1, 0)
    dist = (dil * jnp.abs(rel)).astype(F32)
    scale = dh ** -0.5
    for h in range(ATT_HEADS):
        sl = slice(h * dh, (h + 1) * dh)
        q = q_ref[:, sl].astype(BF16)
        k = jnp.concatenate([kp_ref[tq - half:, sl], kc_ref[:, sl], kn_ref[:half, sl]], axis=0).astype(BF16)
        v = jnp.concatenate([vp_ref[tq - half:, sl], vc_ref[:, sl], vn_ref[:half, sl]], axis=0).astype(BF16)
        s = _dot_nt(q, k) * scale - slopes[h] * dist
        s = jnp.where(ok > 0, s, NEG_BIG)
        mx = jnp.max(s, axis=-1, keepdims=True)
        p = jnp.exp(s - mx)
        den = jnp.sum(p, axis=-1, keepdims=True)
        o_ref[:, sl] = _dot(p.astype(BF16), v) / den
        l_ref[:, sl] = jnp.broadcast_to(mx + jnp.log(den), (tq, dh))


def _attention_group(qkv, g, batch, seq):
    dil = ATT_GROUPS[g][1]
    sub_len = seq // dil
    tq = ATT_TQ
    nq = sub_len // tq
    w = ATT_OUT_WIDTH
    ncb = qkv.shape[1] // w
    n = len(ATT_GROUPS) * ATT_HEADS
    slopes_all = np.power(np.float32(2.0), -8.0 * np.arange(1, n + 1, dtype=np.float32) / n)
    slopes = tuple(float(s) for s in slopes_all[g * ATT_HEADS:(g + 1) * ATT_HEADS])
    x = qkv.reshape(batch, sub_len, dil * qkv.shape[1])
    ng = len(ATT_GROUPS)

    def spec(col, shift):
        def imap(b, p, i):
            return (b, jnp.clip(i + shift, 0, nq - 1), p * ncb + col)
        return pl.BlockSpec((None, tq, w), imap)

    out_spec = pl.BlockSpec((None, tq, w), lambda b, p, i: (b, i, p))
    shape = jax.ShapeDtypeStruct((batch, sub_len, dil * w), F32)
    o, lse = pl.pallas_call(
        functools.partial(_attn_kernel, dil=dil, slopes=slopes, sub_len=sub_len),
        grid=(batch, dil, nq),
        in_specs=[spec(g, 0), spec(ng + g, -1), spec(ng + g, 0), spec(ng + g, 1),
                  spec(2 * ng + g, -1), spec(2 * ng + g, 0), spec(2 * ng + g, 1)],
        out_specs=[out_spec, out_spec],
        out_shape=[shape, shape],
        compiler_params=_params("parallel", "parallel", "arbitrary"),
        name=f"dilated_attn_g{g}",
    )(x, x, x, x, x, x, x)
    return o.reshape(batch * seq, w), lse.reshape(batch * seq, w)


def _att_merge_kernel(o0, o1, o2, l0, l1, l2, y_ref):
    a, b, c = l0[...], l1[...], l2[...]
    mx = jnp.maximum(jnp.maximum(a, b), c)
    ea, eb, ec = jnp.exp(a - mx), jnp.exp(b - mx), jnp.exp(c - mx)
    den = ea + eb + ec
    y_ref[...] = ((ea * o0[...] + eb * o1[...] + ec * o2[...]) / den).astype(y_ref.dtype)


def _att_merge(outs, lses):
    m, w = outs[0].shape
    tm = ROW_TILE
    spec = pl.BlockSpec((tm, w), lambda i: (i, 0))
    return pl.pallas_call(
        _att_merge_kernel,
        grid=(m // tm,),
        in_specs=[spec] * 6,
        out_specs=spec,
        out_shape=jax.ShapeDtypeStruct((m, w), BF16),
        compiler_params=_params("parallel"),
        name="att_merge",
    )(*outs, *lses)


def _ret_kernel(lg_ref, q_ref, k_ref, v_ref, o_ref, s_ref):
    c = RET_CHUNK
    h, z, ci = pl.program_id(1), pl.program_id(2), pl.program_id(3)

    @pl.when(ci == 0)
    def _():
        s_ref[...] = jnp.zeros_like(s_ref)

    q = q_ref[...].astype(BF16)
    k = k_ref[...] * (RET_DK ** -0.5)
    v = v_ref[...].astype(BF16)
    pos = lax.broadcasted_iota(jnp.int32, (c, 1), 0).astype(F32)
    state = s_ref[...]
    cross = _dot(q, state.astype(BF16))

    @pl.when(z == 0)
    def _():
        lgf = jnp.full((1, 1), lg_ref[0, h], F32)
        lgb = jnp.full((1, 1), lg_ref[1, h], F32)
        diff = (lax.broadcasted_iota(jnp.int32, (c, c), 0) - lax.broadcasted_iota(jnp.int32, (c, c), 1)).astype(F32)
        dmat = (jnp.where(diff >= 0, jnp.exp(jnp.maximum(diff, 0.0) * lgf), 0.0)
                + jnp.where(diff <= 0, jnp.exp(jnp.maximum(-diff, 0.0) * lgb), 0.0))
        scores = _dot_nt(q, k.astype(BF16)) * dmat
        o_ref[...] = _dot(scores.astype(BF16), v) + cross * jnp.exp((pos + 1.0) * lgf)
        kz = (k * jnp.exp((c - 1.0 - pos) * lgf)).astype(BF16)
        s_ref[...] = jnp.exp(c * lgf) * state + _dot_tn(kz, v)

    @pl.when(z == 1)
    def _():
        lgb = jnp.full((1, 1), lg_ref[1, h], F32)
        o_ref[...] = cross * jnp.exp((c - pos) * lgb)
        kz = (k * jnp.exp(pos * lgb)).astype(BF16)
        s_ref[...] = jnp.exp(c * lgb) * state + _dot_tn(kz, v)


def _retention(rproj, log_g, batch, seq):
    c = RET_CHUNK
    nc = seq // c
    x = rproj.reshape(batch, seq, rproj.shape[1])

    def chunk(z, ci):
        return ci + z * (nc - 1 - 2 * ci)

    nqk = RET_HEADS * RET_DK // RET_DK
    return pl.pallas_call(
        _ret_kernel,
        grid=(batch, RET_HEADS, 2, nc),
        in_specs=[
            pl.BlockSpec(memory_space=pltpu.SMEM),
            pl.BlockSpec((None, c, RET_DK), lambda b, h, z, ci: (b, chunk(z, ci), h)),
            pl.BlockSpec((None, c, RET_DK), lambda b, h, z, ci: (b, chunk(z, ci), nqk + h)),
            pl.BlockSpec((None, c, RET_DV), lambda b, h, z, ci: (b, chunk(z, ci), nqk + h)),
        ],
        out_specs=pl.BlockSpec((None, None, c, RET_DV), lambda b, h, z, ci: (z, b, chunk(z, ci), h)),
        out_shape=jax.ShapeDtypeStruct((2, batch, seq, RET_HEADS * RET_DV), F32),
        scratch_shapes=[pltpu.VMEM((RET_DK, RET_DV), F32)],
        compiler_params=_params("parallel", "parallel", "arbitrary", "arbitrary"),
        name="retention",
    )(log_g, x, x, x)


def _ret_post_kernel(rf_ref, rb_ref, g_ref, gain_ref, y_ref):
    ret = rf_ref[...] + rb_ref[...]
    gate = g_ref[...]
    gate = gate * _sigmoid(gate)
    gain = gain_ref[...]
    outs = []
    for h in range(RET_HEADS):
        sl = slice(h * RET_DV, (h + 1) * RET_DV)
        y = ret[:, sl]
        mu = jnp.mean(y, axis=-1, keepdims=True)
        yc = y - mu
        var = jnp.mean(yc * yc, axis=-1, keepdims=True)
        outs.append(yc * lax.rsqrt(var + RET_NORM_EPS) * gain[:, sl])
    y_ref[...] = (gate * jnp.concatenate(outs, axis=1)).astype(y_ref.dtype)


def _ret_post(ret2, rproj, gain):
    _, batch, seq, w = ret2.shape
    m = batch * seq
    r = ret2.reshape(2, m, w)
    tm = ROW_TILE
    gcol = rproj.shape[1] // w - 1
    return pl.pallas_call(
        _ret_post_kernel,
        grid=(m // tm,),
        in_specs=[
            pl.BlockSpec((None, tm, w), lambda i: (0, i, 0)),
            pl.BlockSpec((None, tm, w), lambda i: (1, i, 0)),
            pl.BlockSpec((tm, w), lambda i: (i, gcol)),
            pl.BlockSpec((1, w), lambda i: (0, 0)),
        ],
        out_specs=pl.BlockSpec((tm, w), lambda i: (i, 0)),
        out_shape=jax.ShapeDtypeStruct((m, w), BF16),
        compiler_params=_params("parallel"),
        name="ret_post",
    )(r, r, rproj, gain.reshape(1, w))


def _rwkv_prep_kernel(x_ref, xp_ref, xn_ref, conv_ref, w0_ref, w2_ref, a0_ref, a2_ref, g2_ref,
                      rkv_ref, lwa_ref, g_ref, *, ns):
    i = pl.program_id(1)
    x = x_ref[...]
    ts = x.shape[0]
    w = RWKV_WIDTH
    row = lax.broadcasted_iota(jnp.int32, (ts, 1), 0)
    prev_row = xp_ref[7:8, :] * jnp.where(i > 0, 1.0, 0.0)
    next_row = xn_ref[0:1, :] * jnp.where(i < ns - 1, 1.0, 0.0)
    x_m1 = jnp.where(row == 0, prev_row, pltpu.roll(x, 1, 0))
    x_p1 = jnp.where(row == ts - 1, next_row, pltpu.roll(x, ts - 1, 0))
    cw = conv_ref[...]
    feats = cw[0:1, :] * x_m1 + cw[1:2, :] * x + cw[2:3, :] * x_p1
    rkv_ref[...] = feats[:, :3 * w]
    w_lo = jnp.tanh(feats[:, 3 * w:3 * w + 128]).astype(BF16)
    a_lo = feats[:, 3 * w + 128:3 * w + 256].astype(BF16)
    g_lo = _sigmoid(feats[:, 3 * w + 256:3 * w + 384]).astype(BF16)
    logw = -_softplus(-(w0_ref[...] + _dot(w_lo, w2_ref[...]))) - 0.5
    lwa_ref[:, :2 * w] = -jnp.exp(logw)
    lwa_ref[:, 2 * w:] = _sigmoid(a0_ref[...] + _dot(a_lo, a2_ref[...]))
    g_ref[...] = _dot(g_lo, g2_ref[...])


def _rwkv_prep(cfeat, conv, w0, w2bd, a0, a2bd, g2, batch, seq):
    ts = 512
    ns = seq // ts
    ch = cfeat.shape[1]
    w = RWKV_WIDTH
    x = cfeat.reshape(batch, seq, ch)
    full = lambda shape: pl.BlockSpec(shape, lambda b, i: (0, 0))
    return pl.pallas_call(
        functools.partial(_rwkv_prep_kernel, ns=ns),
        grid=(batch, ns),
        in_specs=[
            pl.BlockSpec((None, ts, ch), lambda b, i: (b, i, 0)),
            pl.BlockSpec((None, 8, ch), lambda b, i: (b, jnp.maximum(i * (ts // 8) - 1, 0), 0)),
            pl.BlockSpec((None, 8, ch), lambda b, i: (b, jnp.minimum((i + 1) * (ts // 8), seq // 8 - 1), 0)),
            full((3, ch)), full((1, 2 * w)), full((128, 2 * w)), full((1, 2 * w)), full((128, 2 * w)), full((128, w)),
        ],
        out_specs=[
            pl.BlockSpec((None, ts, 3 * w), lambda b, i: (b, i, 0)),
            pl.BlockSpec((None, ts, 4 * w), lambda b, i: (b, i, 0)),
            pl.BlockSpec((None, ts, w), lambda b, i: (b, i, 0)),
        ],
        out_shape=[
            jax.ShapeDtypeStruct((batch, seq, 3 * w), F32),
            jax.ShapeDtypeStruct((batch, seq, 4 * w), F32),
            jax.ShapeDtypeStruct((batch, seq, w), F32),
        ],
        compiler_params=_params("parallel", "arbitrary"),
        name="rwkv_prep",
    )(x, x, x, conv, w0, w2bd, a0, a2bd, g2)


def _rwkv_chunk_kernel(r_ref, k_ref, v_ref, lw_ref, a_ref, kk_ref, ka_ref, o_ref, s_ref):
    c, n = RWKV_CHUNK, RWKV_N
    z, ci = pl.program_id(1), pl.program_id(2)

    @pl.when(ci == 0)
    def _():
        s_ref[...] = jnp.zeros_like(s_ref)

    sgn = 1 - 2 * z
    d = (lax.broadcasted_iota(jnp.int32, (c, c), 0) - lax.broadcasted_iota(jnp.int32, (c, c), 1)) * sgn
    strict = d > 0
    incl = d >= 0
    tri = jnp.where(incl, 1.0, 0.0).astype(BF16)
    lw = lw_ref[...]
    hi = lw.astype(BF16)
    rem = lw - hi.astype(F32)
    mid = rem.astype(BF16)
    lo = (rem - mid.astype(F32)).astype(BF16)
    cs = _dot(tri, hi) + _dot(tri, mid) + _dot(tri, lo)
    g_all = jnp.exp(jnp.sum(lw, axis=0, keepdims=True))
    e_neg = jnp.exp(-cs)
    e_prev = jnp.exp(cs - lw)
    r, k, v, a = r_ref[...], k_ref[...], v_ref[...], a_ref[...]
    kks = k * kk_ref[...]
    kdir = k * (1.0 + (a - 1.0) * ka_ref[...])
    rbar_all = r * jnp.exp(cs)
    kbar_all = kdir * e_neg
    eye = jnp.where(lax.broadcasted_iota(jnp.int32, (n, n), 0) == lax.broadcasted_iota(jnp.int32, (n, n), 1), 1.0, 0.0)
    n_sq = c.bit_length() - 1
    outs = []
    for h in range(RWKV_HEADS):
        sl = slice(h * n, (h + 1) * n)
        kk = kks[:, sl]
        kk = kk * lax.rsqrt(jnp.sum(kk * kk, axis=-1, keepdims=True) + 1e-12)
        abar = -kk * e_prev[:, sl]
        bbar = kk * a[:, sl] * e_neg[:, sl]
        kbar = kbar_all[:, sl]
        rbar = rbar_all[:, sl]
        vh = v[:, sl]
        vb = vh.astype(BF16)
        bk = jnp.concatenate([bbar, kbar], axis=0).astype(BF16)
        x = _dot_nt(jnp.concatenate([abar, rbar], axis=0).astype(BF16), bk)
        l_b = jnp.where(strict, x[:c, :c], 0.0)
        l_k = jnp.where(strict, x[:c, c:], 0.0)
        m_b = jnp.where(incl, x[c:, :c], 0.0)
        m_k = jnp.where(incl, x[c:, c:], 0.0)
        zz = jnp.concatenate([abar, _dot(l_k.astype(BF16), vb)], axis=1)
        pw = l_b
        for it in range(n_sq):
            pwb = pw.astype(BF16)
            zz = zz + _dot(pwb, zz.astype(BF16))
            if it < n_sq - 1:
                pw = _dot(pwb, pwb)
        wmat = zz[:, :n]
        uloc = zz[:, n:]
        wb = wmat.astype(BF16)
        m_bb = m_b.astype(BF16)
        uv = jnp.concatenate([uloc, vh], axis=0).astype(BF16)
        qt = rbar + _dot(m_bb, wb)
        yloc = _dot(jnp.concatenate([m_bb, m_k.astype(BF16)], axis=1), uv)
        state = s_ref[h]
        s_hi, s_lo = _split2(state)
        q_hi, q_lo = _split2(qt)
        outs.append(_dot_nt(q_hi, s_hi) + _dot_nt(q_hi, s_lo) + _dot_nt(q_lo, s_hi) + yloc)
        gh = g_all[:, sl]
        gmat = _dot_tn(uv, bk) * gh
        pmat = (eye + _dot_tn(wb, bbar.astype(BF16))) * gh
        p_hi, p_lo = _split2(pmat)
        s_ref[h] = _dot(s_hi, p_hi) + _dot(s_hi, p_lo) + _dot(s_lo, p_hi) + gmat
    o_ref[...] = jnp.concatenate(outs, axis=1)


def _rwkv_scan(rkv, lwa, kk_scale, k_a, batch, seq):
    c = RWKV_CHUNK
    nc = seq // c
    w = RWKV_WIDTH

    def chunk(z, ci):
        return ci + z * (nc - 1 - 2 * ci)

    def tok(col):
        return pl.BlockSpec((None, c, w), lambda b, z, ci: (b, chunk(z, ci), col))

    vec = pl.BlockSpec((1, w), lambda b, z, ci: (0, 0))
    return pl.pallas_call(
        _rwkv_chunk_kernel,
        grid=(batch, 2, nc),
        in_specs=[
            tok(0), tok(1), tok(2),
            pl.BlockSpec((None, c, w), lambda b, z, ci: (b, chunk(z, ci), z)),
            pl.BlockSpec((None, c, w), lambda b, z, ci: (b, chunk(z, ci), 2 + z)),
            vec, vec,
        ],
        out_specs=pl.BlockSpec((None, None, c, w), lambda b, z, ci: (z, b, chunk(z, ci), 0)),
        out_shape=jax.ShapeDtypeStruct((2, batch, seq, w), F32),
        scratch_shapes=[pltpu.VMEM((RWKV_HEADS, RWKV_N, RWKV_N), F32)],
        compiler_params=_params("parallel", "arbitrary", "arbitrary"),
        name="rwkv_scan",
    )(rkv, rkv, rkv, lwa, lwa, kk_scale, k_a)


def _rwkv_post_kernel(yf_ref, yb_ref, r_ref, k_ref, v_ref, a0_ref, a1_ref, g_ref, ka_ref, rk_ref, lnw_ref, lnb_ref, o_ref):
    n = RWKV_N
    y = yf_ref[...] + yb_ref[...]
    k, v, ka = k_ref[...], v_ref[...], ka_ref[...]
    kd = k * (1.0 + (a0_ref[...] - 1.0) * ka) + k * (1.0 + (a1_ref[...] - 1.0) * ka)
    t = r_ref[...] * kd * rk_ref[...]
    lnw, lnb = lnw_ref[...], lnb_ref[...]
    outs = []
    for h in range(RWKV_HEADS):
        sl = slice(h * n, (h + 1) * n)
        yh = y[:, sl]
        mu = jnp.mean(yh, axis=-1, keepdims=True)
        yc = yh - mu
        var = jnp.mean(yc * yc, axis=-1, keepdims=True)
        bonus = jnp.sum(t[:, sl], axis=-1, keepdims=True)
        outs.append(yc * lax.rsqrt(var + RWKV_NORM_EPS) * lnw[:, sl] + lnb[:, sl] + bonus * v[:, sl])
    o_ref[...] = (jnp.concatenate(outs, axis=1) * g_ref[...]).astype(o_ref.dtype)


def _rwkv_post(ys, rkv, lwa, g, k_a, r_k, ln_w, ln_b):
    _, batch, seq, w = ys.shape
    m = batch * seq
    tm = ROW_TILE
    y2 = ys.reshape(2, m, w)
    rkv2 = rkv.reshape(m, 3 * w)
    lwa2 = lwa.reshape(m, 4 * w)
    tok = lambda col: pl.BlockSpec((tm, w), lambda i: (i, col))
    vec = pl.BlockSpec((1, w), lambda i: (0, 0))
    return pl.pallas_call(
        _rwkv_post_kernel,
        grid=(m // tm,),
        in_specs=[
            pl.BlockSpec((None, tm, w), lambda i: (0, i, 0)),
            pl.BlockSpec((None, tm, w), lambda i: (1, i, 0)),
            tok(0), tok(1), tok(2), tok(2), tok(3), tok(0), vec, vec, vec, vec,
        ],
        out_specs=pl.BlockSpec((tm, w), lambda i: (i, 0)),
        out_shape=jax.ShapeDtypeStruct((m, w), BF16),
        compiler_params=_params("parallel"),
        name="rwkv_post",
    )(y2, y2, rkv2, rkv2, rkv2, lwa2, lwa2, g.reshape(m, w), k_a, r_k, ln_w, ln_b)


def _merge_kernel(ya_ref, yb_ref, yc_ref, ga_ref, gb_ref, gc_ref, wa_ref, wb_ref, wc_ref, o_ref):
    acc = _sigmoid(ga_ref[...]) * _dot(ya_ref[...], wa_ref[...])
    acc += _sigmoid(gb_ref[...]) * _dot(yb_ref[...], wb_ref[...])
    acc += _sigmoid(gc_ref[...]) * _dot(yc_ref[...], wc_ref[...])
    o_ref[...] = acc.astype(o_ref.dtype)


def _merge(ya, yb, yc, gates, wa, wb, wc):
    m = ya.shape[0]
    d = wa.shape[1]
    tm, tn = ROW_TILE, 512
    nj = d // tn
    act = lambda a: pl.BlockSpec((tm, a.shape[1]), lambda i, j: (i, 0))
    wsp = lambda a: pl.BlockSpec((a.shape[0], tn), lambda i, j: (0, j))
    gsp = lambda b: pl.BlockSpec((tm, tn), lambda i, j: (i, b * nj + j))
    return pl.pallas_call(
        _merge_kernel,
        grid=(m // tm, nj),
        in_specs=[act(ya), act(yb), act(yc), gsp(0), gsp(1), gsp(2), wsp(wa), wsp(wb), wsp(wc)],
        out_specs=pl.BlockSpec((tm, tn), lambda i, j: (i, j)),
        out_shape=jax.ShapeDtypeStruct((m, d), BF16),
        compiler_params=_params("parallel", "arbitrary"),
        name="branch_merge",
    )(ya, yb, yc, gates, gates, gates, wa, wb, wc)


def _out_proj_kernel(x_ref, m_ref, w_ref, o_ref):
    o_ref[...] = x_ref[...] + _dot(m_ref[...], w_ref[...])


def _out_proj(x, merged, w):
    m, d = x.shape
    tm, tn = ROW_TILE, 512
    return pl.pallas_call(
        _out_proj_kernel,
        grid=(m // tm, d // tn),
        in_specs=[
            pl.BlockSpec((tm, tn), lambda i, j: (i, j)),
            pl.BlockSpec((tm, d), lambda i, j: (i, 0)),
            pl.BlockSpec((d, tn), lambda i, j: (0, j)),
        ],
        out_specs=pl.BlockSpec((tm, tn), lambda i, j: (i, j)),
        out_shape=jax.ShapeDtypeStruct((m, d), F32),
        compiler_params=_params("parallel", "arbitrary"),
        name="out_proj",
    )(x, merged, w)


def _block_diag2(w2):
    zero = jnp.zeros_like(w2[0])
    return jnp.concatenate([jnp.concatenate([w2[0], zero], axis=1), jnp.concatenate([zero, w2[1]], axis=1)], axis=0)


def _mixers(x, p, batch, seq):
    qkv = _norm_matmul(x, p["mix_norm"], p["w_qkv"], 512)
    rproj = _norm_matmul(x, p["mix_norm"], p["w_ret"], 512)
    cfeat = _norm_matmul(x, p["mix_norm"], p["w_rwkv"], 640)
    gates = _norm_matmul(x, p["mix_norm"], p["w_gates"], 512)

    outs, lses = [], []
    for g in range(len(ATT_GROUPS)):
        o, l = _attention_group(qkv, g, batch, seq)
        outs.append(o)
        lses.append(l)
    ya = _att_merge(outs, lses)

    ret2 = _retention(rproj, p["ret_log_g"], batch, seq)
    yb = _ret_post(ret2, rproj, p["ret_norm"])

    rkv, lwa, g = _rwkv_prep(cfeat, p["rwkv_conv"], p["rwkv_w0"], p["rwkv_w2bd"], p["rwkv_a0"], p["rwkv_a2bd"],
                             p["rwkv_g2"], batch, seq)
    ys = _rwkv_scan(rkv, lwa, p["rwkv_k_k"], p["rwkv_k_a"], batch, seq)
    yc = _rwkv_post(ys, rkv, lwa, g, p["rwkv_k_a"], p["rwkv_r_k"], p["rwkv_ln_w"], p["rwkv_ln_b"])

    merged = _merge(ya, yb, yc, gates, p["w_branch_a"], p["w_branch_b"], p["w_branch_c"])
    return _out_proj(x, merged, p["w_out"])


def kernel(x_prompt, x_sample, ffn1_norm, ffn1_w_gate, ffn1_w_up, ffn1_w_down, mix_norm, w_in, ret_decay_logit, ret_norm, rwkv_conv, rwkv_w0, rwkv_w2, rwkv_a0, rwkv_a2, rwkv_g2, rwkv_k_k, rwkv_k_a, rwkv_r_k, rwkv_ln_w, rwkv_ln_b, w_branch_a, w_branch_b, w_branch_c, w_out, ffn2_norm, ffn2_w_gate, ffn2_w_up, ffn2_w_down, final_norm):
    depth = ffn1_norm.shape[0]
    w = RWKV_WIDTH
    att3 = 3 * len(ATT_GROUPS) * ATT_OUT_WIDTH
    ret_w = 2 * RET_HEADS * RET_DK + 2 * RET_HEADS * RET_DV
    layers = []
    for l in range(depth):
        w_in_l = w_in[l].astype(BF16)
        o1, o2, o3 = att3, att3 + ret_w, att3 + ret_w + RWKV_CONV_CH
        layers.append({
            "ffn1": (ffn1_norm[l], ffn1_w_gate[l].astype(BF16), ffn1_w_up[l].astype(BF16), ffn1_w_down[l].astype(BF16)),
            "ffn2": (ffn2_norm[l], ffn2_w_gate[l].astype(BF16), ffn2_w_up[l].astype(BF16), ffn2_w_down[l].astype(BF16)),
            "mix_norm": mix_norm[l],
            "w_qkv": w_in_l[:, :o1], "w_ret": w_in_l[:, o1:o2], "w_rwkv": w_in_l[:, o2:o3], "w_gates": w_in_l[:, o3:],
            "ret_log_g": jax.nn.log_sigmoid(ret_decay_logit[l].astype(F32)),
            "ret_norm": ret_norm[l].astype(F32),
            "rwkv_conv": rwkv_conv[l].astype(F32),
            "rwkv_w0": rwkv_w0[l].astype(F32).reshape(1, 2 * w),
            "rwkv_w2bd": _block_diag2(rwkv_w2[l]).astype(BF16),
            "rwkv_a0": rwkv_a0[l].astype(F32).reshape(1, 2 * w),
            "rwkv_a2bd": _block_diag2(rwkv_a2[l]).astype(BF16),
            "rwkv_g2": rwkv_g2[l].astype(BF16),
            "rwkv_k_k": rwkv_k_k[l].astype(F32).reshape(1, w),
            "rwkv_k_a": rwkv_k_a[l].astype(F32).reshape(1, w),
            "rwkv_r_k": rwkv_r_k[l].astype(F32).reshape(1, w),
            "rwkv_ln_w": rwkv_ln_w[l].astype(F32).reshape(1, w),
            "rwkv_ln_b": rwkv_ln_b[l].astype(F32).reshape(1, w),
            "w_branch_a": w_branch_a[l].astype(BF16), "w_branch_b": w_branch_b[l].astype(BF16),
            "w_branch_c": w_branch_c[l].astype(BF16), "w_out": w_out[l].astype(BF16),
        })

    def run(x):
        batch, seq, d = x.shape
        y = x.reshape(batch * seq, d)
        for l, p in enumerate(layers):
            y = _ffn(y, *p["ffn1"], final_norm, False)
            y = _mixers(y, p, batch, seq)
            y = _ffn(y, *p["ffn2"], final_norm, l == depth - 1)
        return y.reshape(batch, seq, d)

    return (run(x_prompt), run(x_sample))
```

```python
import functools

import numpy as np
import jax
import jax.numpy as jnp
from jax import lax
from jax.experimental import pallas as pl
from jax.experimental.pallas import tpu as pltpu

F32 = jnp.float32
BF16 = jnp.bfloat16

NORM_EPS = 1e-6

ATT_GROUPS = ((128, 1), (512, 4), (2048, 16))
ATT_HEADS = 4
ATT_HEAD_DIM = 128
ATT_OUT_WIDTH = ATT_HEADS * ATT_HEAD_DIM
ATT_HALF = 64
ATT_TQ = 128
ATT_LSE_LANES = 128 // ATT_HEADS

RET_HEADS = 4
RET_DK = 128
RET_DV = 256
RET_CHUNK = 128
RET_NORM_EPS = 1e-5

RWKV_HEADS = 8
RWKV_N = 64
RWKV_WIDTH = RWKV_HEADS * RWKV_N
RWKV_CONV_CH = 1920
RWKV_NORM_EPS = 64e-5
RWKV_CHUNK = 64
RWKV_GROUP_HEADS = 4

ROW_TILE = 512
MM_ROW_TILE = 1024
VMEM_LIMIT_BYTES = 56 * 1024 * 1024
NEG_BIG = -1e30


def _params(*sem):
    return pltpu.CompilerParams(dimension_semantics=sem, vmem_limit_bytes=VMEM_LIMIT_BYTES)


def _dot(a, b):
    return jnp.dot(a, b, preferred_element_type=F32)


def _dot_nt(a, b):
    return lax.dot_general(a, b, (((1,), (1,)), ((), ())), preferred_element_type=F32)


def _dot_tn(a, b):
    return lax.dot_general(a, b, (((0,), (0,)), ((), ())), preferred_element_type=F32)


def _split2(a):
    hi = a.astype(BF16)
    lo = (a - hi.astype(F32)).astype(BF16)
    return hi, lo


def _rms(x, gain):
    ms = jnp.mean(x * x, axis=-1, keepdims=True)
    return x * lax.rsqrt(ms + NORM_EPS) * gain


def _sigmoid(x):
    return 1.0 / (1.0 + jnp.exp(-x))


def _softplus(x):
    return jnp.maximum(x, 0.0) + jnp.log(1.0 + jnp.exp(-jnp.abs(x)))


def _ffn_kernel(x_ref, g_ref, wg_ref, wu_ref, wd_ref, g2_ref, o_ref, *rest, nf, mode):
    xn_ref = rest[-1]
    f = pl.program_id(1)

    @pl.when(f == 0)
    def _():
        xn_ref[...] = _rms(x_ref[...], g_ref[...]).astype(BF16)
        o_ref[...] = jnp.zeros_like(o_ref)

    xn = xn_ref[...]
    hg = _dot(xn, wg_ref[...])
    hu = _dot(xn, wu_ref[...])
    h = (hg * _sigmoid(hg) * hu).astype(BF16)
    o_ref[...] += _dot(h, wd_ref[...])

    @pl.when(f == nf - 1)
    def _():
        y = x_ref[...] + 0.5 * o_ref[...]
        if mode == "final":
            y = _rms(y, g2_ref[...])
        o_ref[...] = y
        if mode == "next_norm":
            rest[0][...] = _rms(y, g2_ref[...]).astype(BF16)


def _ffn(x, gain, wg, wu, wd, gain2, mode):
    m, d = x.shape
    ff = wg.shape[1]
    tm, tf = ROW_TILE, 512
    nf = ff // tf
    row = pl.BlockSpec((tm, d), lambda i, f: (i, 0))
    vec = pl.BlockSpec((1, d), lambda i, f: (0, 0))
    out_specs, out_shape = [row], [jax.ShapeDtypeStruct((m, d), F32)]
    if mode == "next_norm":
        out_specs.append(row)
        out_shape.append(jax.ShapeDtypeStruct((m, d), BF16))
    return pl.pallas_call(
        functools.partial(_ffn_kernel, nf=nf, mode=mode),
        grid=(m // tm, nf),
        in_specs=[row, vec,
                  pl.BlockSpec((d, tf), lambda i, f: (0, f)),
                  pl.BlockSpec((d, tf), lambda i, f: (0, f)),
                  pl.BlockSpec((tf, d), lambda i, f: (f, 0)),
                  vec],
        out_specs=out_specs,
        out_shape=out_shape,
        scratch_shapes=[pltpu.VMEM((tm, d), BF16)],
        compiler_params=_params("parallel", "arbitrary"),
        name="ffn",
    )(x, gain.reshape(1, d), wg, wu, wd, gain2.reshape(1, d))


def _matmul_kernel(x_ref, w_ref, o_ref):
    o_ref[...] = _dot(x_ref[...], w_ref[...]).astype(o_ref.dtype)


def _matmul(x, w, tn, out_dtype):
    m, d = x.shape
    n = w.shape[1]
    tm = MM_ROW_TILE
    return pl.pallas_call(
        _matmul_kernel,
        grid=(m // tm, n // tn),
        in_specs=[pl.BlockSpec((tm, d), lambda i, j: (i, 0)), pl.BlockSpec((d, tn), lambda i, j: (0, j))],
        out_specs=pl.BlockSpec((tm, tn), lambda i, j: (i, j)),
        out_shape=jax.ShapeDtypeStruct((m, n), out_dtype),
        compiler_params=_params("parallel", "arbitrary"),
        name="proj",
    )(x, w)


def _attn_kernel(q_ref, kp_ref, kc_ref, kn_ref, vp_ref, vc_ref, vn_ref, o_ref, l_ref, *, dil, slopes, sub_len):
    tq, half, dh = ATT_TQ, ATT_HALF, ATT_HEAD_DIM
    i = pl.program_id(2)
    qi = lax.broadcasted_iota(jnp.int32, (tq, tq + 2 * half), 0)
    kj = lax.broadcasted_iota(jnp.int32, (tq, tq + 2 * half), 1)
    rel = kj - half - qi
    kidx = i * tq - half + kj
    ok = jnp.where(jnp.abs(rel) <= half, 1, 0) * jnp.where(kidx >= 0, 1, 0) * jnp.where(kidx < sub_len, 1, 0)
    dist = (dil * jnp.abs(rel)).astype(F32)
    scale = dh ** -0.5
    for h in range(ATT_HEADS):
        sl = slice(h * dh, (h + 1) * dh)
        k = jnp.concatenate([kp_ref[tq - half:, sl], kc_ref[:, sl], kn_ref[:half, sl]], axis=0)
        v = jnp.concatenate([vp_ref[tq - half:, sl], vc_ref[:, sl], vn_ref[:half, sl]], axis=0)
        s = _dot_nt(q_ref[:, sl], k) * scale - slopes[h] * dist
        s = jnp.where(ok > 0, s, NEG_BIG)
        mx = jnp.max(s, axis=-1, keepdims=True)
        p = jnp.exp(s - mx)
        den = jnp.sum(p, axis=-1, keepdims=True)
        o_ref[:, sl] = _dot(p.astype(BF16), v) / den
        ll = ATT_LSE_LANES
        l_ref[:, h * ll:(h + 1) * ll] = jnp.broadcast_to(mx + jnp.log(den), (tq, ll))


def _attention_group(qkv, g, batch, seq):
    dil = ATT_GROUPS[g][1]
    sub_len = seq // dil
    tq = ATT_TQ
    nq = sub_len // tq
    w = ATT_OUT_WIDTH
    n = len(ATT_GROUPS) * ATT_HEADS
    slopes_all = np.power(np.float32(2.0), -8.0 * np.arange(1, n + 1, dtype=np.float32) / n)
    slopes = tuple(float(s) for s in slopes_all[g * ATT_HEADS:(g + 1) * ATT_HEADS])
    x = qkv.reshape(batch, sub_len, dil * 3 * w)

    def spec(col, shift):
        def imap(b, p, i):
            return (b, jnp.clip(i + shift, 0, nq - 1), p * 3 + col)
        return pl.BlockSpec((None, tq, w), imap)

    o, lse = pl.pallas_call(
        functools.partial(_attn_kernel, dil=dil, slopes=slopes, sub_len=sub_len),
        grid=(batch, dil, nq),
        in_specs=[spec(0, 0), spec(1, -1), spec(1, 0), spec(1, 1), spec(2, -1), spec(2, 0), spec(2, 1)],
        out_specs=[pl.BlockSpec((None, tq, w), lambda b, p, i: (b, i, p)),
                   pl.BlockSpec((None, tq, 128), lambda b, p, i: (b, i, p))],
        out_shape=[jax.ShapeDtypeStruct((batch, sub_len, dil * w), F32),
                   jax.ShapeDtypeStruct((batch, sub_len, dil * 128), F32)],
        compiler_params=_params("parallel", "parallel", "arbitrary"),
        name=f"dilated_attn_g{g}",
    )(x, x, x, x, x, x, x)
    return o.reshape(batch * seq, w), lse.reshape(batch * seq, 128)


def _att_merge_kernel(o0, o1, o2, l0, l1, l2, y_ref):
    a, b, c = l0[...], l1[...], l2[...]
    mx = jnp.maximum(jnp.maximum(a, b), c)
    ea, eb, ec = jnp.exp(a - mx), jnp.exp(b - mx), jnp.exp(c - mx)
    inv = 1.0 / (ea + eb + ec)
    wa, wb, wc = ea * inv, eb * inv, ec * inv
    dh, ll = ATT_HEAD_DIM, ATT_LSE_LANES
    tm = a.shape[0]
    for h in range(ATT_HEADS):
        sl = slice(h * dh, (h + 1) * dh)
        bc = lambda t: jnp.broadcast_to(t[:, h * ll:h * ll + 1], (tm, dh))
        y_ref[:, sl] = (bc(wa) * o0[:, sl] + bc(wb) * o1[:, sl] + bc(wc) * o2[:, sl]).astype(y_ref.dtype)


def _att_merge(outs, lses):
    m, w = outs[0].shape
    tm = ROW_TILE
    spec = pl.BlockSpec((tm, w), lambda i: (i, 0))
    lspec = pl.BlockSpec((tm, 128), lambda i: (i, 0))
    return pl.pallas_call(
        _att_merge_kernel,
        grid=(m // tm,),
        in_specs=[spec] * 3 + [lspec] * 3,
        out_specs=spec,
        out_shape=jax.ShapeDtypeStruct((m, w), BF16),
        compiler_params=_params("parallel"),
        name="att_merge",
    )(*outs, *lses)


def _ret_kernel(lg_ref, qf_ref, kf_ref, vf_ref, qb_ref, kb_ref, vb_ref, of_ref, ob_ref, sf_ref, sb_ref):
    c, dk, dv = RET_CHUNK, RET_DK, RET_DV

    @pl.when(pl.program_id(1) == 0)
    def _():
        sf_ref[...] = jnp.zeros_like(sf_ref)
        sb_ref[...] = jnp.zeros_like(sb_ref)

    pos = lax.broadcasted_iota(jnp.int32, (c, 1), 0).astype(F32)
    diff = (lax.broadcasted_iota(jnp.int32, (c, c), 0) - lax.broadcasted_iota(jnp.int32, (c, c), 1)).astype(F32)
    up = jnp.maximum(diff, 0.0)
    dn = jnp.maximum(-diff, 0.0)
    for h in range(RET_HEADS):
        ks, vs = slice(h * dk, (h + 1) * dk), slice(h * dv, (h + 1) * dv)
        lgf = jnp.full((1, 1), lg_ref[0, h], F32)
        lgb = jnp.full((1, 1), lg_ref[1, h], F32)

        q = qf_ref[:, ks].astype(BF16)
        k = kf_ref[:, ks] * (dk ** -0.5)
        v = vf_ref[:, vs].astype(BF16)
        state = sf_ref[h]
        dmat = jnp.where(diff >= 0, jnp.exp(up * lgf), 0.0) + jnp.where(diff <= 0, jnp.exp(dn * lgb), 0.0)
        scores = _dot_nt(q, k.astype(BF16)) * dmat
        of_ref[:, vs] = _dot(scores.astype(BF16), v) + _dot(q, state.astype(BF16)) * jnp.exp((pos + 1.0) * lgf)
        kz = (k * jnp.exp((c - 1.0 - pos) * lgf)).astype(BF16)
        sf_ref[h] = jnp.exp(c * lgf) * state + _dot_tn(kz, v)

        q = qb_ref[:, ks].astype(BF16)
        k = kb_ref[:, ks] * (dk ** -0.5)
        v = vb_ref[:, vs].astype(BF16)
        state = sb_ref[h]
        ob_ref[:, vs] = _dot(q, state.astype(BF16)) * jnp.exp((c - pos) * lgb)
        kz = (k * jnp.exp(pos * lgb)).astype(BF16)
        sb_ref[h] = jnp.exp(c * lgb) * state + _dot_tn(kz, v)


def _retention(rproj, log_g, batch, seq):
    c = RET_CHUNK
    nc = seq // c
    qk_w, v_w = RET_HEADS * RET_DK, RET_HEADS * RET_DV
    x = rproj.reshape(batch, seq, rproj.shape[1])

    def tok(width, col, rev):
        return pl.BlockSpec((None, c, width), lambda b, ci: (b, nc - 1 - ci if rev else ci, col))

    shape = jax.ShapeDtypeStruct((batch, seq, v_w), F32)
    state = pltpu.VMEM((RET_HEADS, RET_DK, RET_DV), F32)
    v_col = 2 * qk_w // v_w
    return pl.pallas_call(
        _ret_kernel,
        grid=(batch, nc),
        in_specs=[pl.BlockSpec(memory_space=pltpu.SMEM),
                  tok(qk_w, 0, False), tok(qk_w, 1, False), tok(v_w, v_col, False),
                  tok(qk_w, 0, True), tok(qk_w, 1, True), tok(v_w, v_col, True)],
        out_specs=[tok(v_w, 0, False), tok(v_w, 0, True)],
        out_shape=[shape, shape],
        scratch_shapes=[state, state],
        compiler_params=_params("parallel", "arbitrary"),
        name="retention",
    )(log_g, x, x, x, x, x, x)


def _ret_post_kernel(rf_ref, rb_ref, g_ref, gain_ref, y_ref):
    ret = rf_ref[...] + rb_ref[...]
    gate = g_ref[...]
    gate = gate * _sigmoid(gate)
    gain = gain_ref[...]
    for h in range(RET_HEADS):
        sl = slice(h * RET_DV, (h + 1) * RET_DV)
        y = ret[:, sl]
        mu = jnp.mean(y, axis=-1, keepdims=True)
        yc = y - mu
        var = jnp.mean(yc * yc, axis=-1, keepdims=True)
        y_ref[:, sl] = (gate[:, sl] * yc * lax.rsqrt(var + RET_NORM_EPS) * gain[:, sl]).astype(y_ref.dtype)


def _ret_post(ret_f, ret_b, rproj, gain):
    batch, seq, w = ret_f.shape
    m = batch * seq
    tm = ROW_TILE
    gcol = rproj.shape[1] // w - 1
    row = pl.BlockSpec((tm, w), lambda i: (i, 0))
    return pl.pallas_call(
        _ret_post_kernel,
        grid=(m // tm,),
        in_specs=[row, row, pl.BlockSpec((tm, w), lambda i: (i, gcol)), pl.BlockSpec((1, w), lambda i: (0, 0))],
        out_specs=row,
        out_shape=jax.ShapeDtypeStruct((m, w), BF16),
        compiler_params=_params("parallel"),
        name="ret_post",
    )(ret_f.reshape(m, w), ret_b.reshape(m, w), rproj, gain.reshape(1, w))


def _rwkv_prep_kernel(x_ref, xp_ref, xn_ref, conv_ref, w0_ref, w2_ref, a0_ref, a2_ref, g2_ref,
                      rkv_ref, lwa_ref, g_ref, *, ns):
    i = pl.program_id(1)
    x = x_ref[...]
    ts = x.shape[0]
    w = RWKV_WIDTH
    row = lax.broadcasted_iota(jnp.int32, (ts, 1), 0)
    prev_row = xp_ref[7:8, :] * jnp.where(i > 0, 1.0, 0.0)
    next_row = xn_ref[0:1, :] * jnp.where(i < ns - 1, 1.0, 0.0)
    x_m1 = jnp.where(row == 0, prev_row, pltpu.roll(x, 1, 0))
    x_p1 = jnp.where(row == ts - 1, next_row, pltpu.roll(x, ts - 1, 0))
    cw = conv_ref[...]
    feats = cw[0:1, :] * x_m1 + cw[1:2, :] * x + cw[2:3, :] * x_p1
    rkv_ref[...] = feats[:, :3 * w]
    w_lo = jnp.tanh(feats[:, 3 * w:3 * w + 128]).astype(BF16)
    a_lo = feats[:, 3 * w + 128:3 * w + 256].astype(BF16)
    g_lo = _sigmoid(feats[:, 3 * w + 256:3 * w + 384]).astype(BF16)
    logw = -_softplus(-(w0_ref[...] + _dot(w_lo, w2_ref[...]))) - 0.5
    lwa_ref[:, :2 * w] = -jnp.exp(logw)
    lwa_ref[:, 2 * w:] = _sigmoid(a0_ref[...] + _dot(a_lo, a2_ref[...]))
    g_ref[...] = _dot(g_lo, g2_ref[...])


def _rwkv_prep(cfeat, conv, w0, w2bd, a0, a2bd, g2, batch, seq):
    ts = 512
    ns = seq // ts
    ch = cfeat.shape[1]
    w = RWKV_WIDTH
    x = cfeat.reshape(batch, seq, ch)
    full = lambda shape: pl.BlockSpec(shape, lambda b, i: (0, 0))
    return pl.pallas_call(
        functools.partial(_rwkv_prep_kernel, ns=ns),
        grid=(batch, ns),
        in_specs=[
            pl.BlockSpec((None, ts, ch), lambda b, i: (b, i, 0)),
            pl.BlockSpec((None, 8, ch), lambda b, i: (b, jnp.maximum(i * (ts // 8) - 1, 0), 0)),
            pl.BlockSpec((None, 8, ch), lambda b, i: (b, jnp.minimum((i + 1) * (ts // 8), seq // 8 - 1), 0)),
            full((3, ch)), full((1, 2 * w)), full((128, 2 * w)), full((1, 2 * w)), full((128, 2 * w)), full((128, w)),
        ],
        out_specs=[
            pl.BlockSpec((None, ts, 3 * w), lambda b, i: (b, i, 0)),
            pl.BlockSpec((None, ts, 4 * w), lambda b, i: (b, i, 0)),
            pl.BlockSpec((None, ts, w), lambda b, i: (b, i, 0)),
        ],
        out_shape=[
            jax.ShapeDtypeStruct((batch, seq, 3 * w), F32),
            jax.ShapeDtypeStruct((batch, seq, 4 * w), F32),
            jax.ShapeDtypeStruct((batch, seq, w), F32),
        ],
        compiler_params=_params("parallel", "arbitrary"),
        name="rwkv_prep",
    )(x, x, x, conv, w0, w2bd, a0, a2bd, g2)


def _rwkv_dir(r, k, v, lw, a, kk_scale, k_a, ones_bd, s_ref, reverse):
    c, n, gh = RWKV_CHUNK, RWKV_N, RWKV_GROUP_HEADS
    gw = gh * n
    sgn = -1 if reverse else 1
    d = (lax.broadcasted_iota(jnp.int32, (c, c), 0) - lax.broadcasted_iota(jnp.int32, (c, c), 1)) * sgn
    tri = jnp.where(d >= 0, 1.0, 0.0).astype(BF16)
    hi = lw.astype(BF16)
    rem = lw - hi.astype(F32)
    mid = rem.astype(BF16)
    lo = (rem - mid.astype(F32)).astype(BF16)
    cs = _dot(tri, hi) + _dot(tri, mid) + _dot(tri, lo)
    g_all = jnp.exp(jnp.sum(lw, axis=0, keepdims=True))
    e_neg = jnp.exp(-cs)
    kks = k * kk_scale
    sq_hi, sq_lo = _split2(kks * kks)
    kk = kks * lax.rsqrt(_dot(sq_hi, ones_bd) + _dot(sq_lo, ones_bd) + 1e-12)
    abar_all = -kk * jnp.exp(cs - lw)
    bbar_all = kk * a * e_neg
    kbar_all = k * (1.0 + (a - 1.0) * k_a) * e_neg
    rbar_all = r * jnp.exp(cs)

    row = lax.broadcasted_iota(jnp.int32, (gw, gw), 0)
    col = lax.broadcasted_iota(jnp.int32, (gw, gw), 1)
    same_head = (row // n) == (col // n)
    dd = jnp.where(same_head, (row - col) * sgn, -1)
    strict = dd > 0
    incl = dd >= 0
    eye = jnp.where(row == col, 1.0, 0.0)
    tile = lambda x: jnp.concatenate([x] * gh, axis=0)
    n_sq = c.bit_length() - 1
    outs = []
    for g in range(RWKV_HEADS // gh):
        gl = slice(g * gw, (g + 1) * gw)
        a_s = jnp.where(same_head, tile(abar_all[:, gl]), 0.0)
        r_s = jnp.where(same_head, tile(rbar_all[:, gl]), 0.0)
        v_s = jnp.where(same_head, tile(v[:, gl]), 0.0).astype(BF16)
        b_t = tile(bbar_all[:, gl].astype(BF16))
        k_t = tile(kbar_all[:, gl].astype(BF16))
        bk = jnp.concatenate([b_t, k_t], axis=0)
        f = _dot_nt(jnp.concatenate([a_s, r_s], axis=0).astype(BF16), bk)
        l_b = jnp.where(strict, f[:gw, :gw], 0.0)
        l_k = jnp.where(strict, f[:gw, gw:], 0.0).astype(BF16)
        m_b = jnp.where(incl, f[gw:, :gw], 0.0).astype(BF16)
        m_k = jnp.where(incl, f[gw:, gw:], 0.0).astype(BF16)
        tt = eye + l_b
        pw = l_b
        for _ in range(n_sq - 1):
            pwb = pw.astype(BF16)
            pw = _dot(pwb, pwb)
            tt = tt + _dot(pw.astype(BF16), tt.astype(BF16))
        zz = _dot(tt.astype(BF16), jnp.concatenate([a_s.astype(BF16), _dot(l_k, v_s).astype(BF16)], axis=1))
        w_b = zz[:, :gw].astype(BF16)
        uv = jnp.concatenate([zz[:, gw:].astype(BF16), v_s], axis=0)
        qt = r_s + _dot(m_b, w_b)
        yloc = _dot(jnp.concatenate([m_b, m_k], axis=1), uv)
        state = s_ref[g]
        s_b = state.astype(BF16)
        y_s = _dot_nt(qt.astype(BF16), s_b) + yloc
        y = y_s[0:c]
        for i in range(1, gh):
            y = y + y_s[i * c:(i + 1) * c]
        outs.append(y)
        gdec = g_all[:, gl]
        gmat = jnp.where(same_head, _dot_tn(uv, bk), 0.0) * gdec
        p_off = jnp.where(same_head, _dot_tn(w_b, b_t), 0.0) * gdec
        s_ref[g] = state * gdec + _dot(s_b, p_off.astype(BF16)) + gmat
    return jnp.concatenate(outs, axis=1)


def _rwkv_chunk_kernel(rf_ref, kf_ref, vf_ref, lwf_ref, af_ref, rb_ref, kb_ref, vb_ref, lwb_ref, ab_ref,
                       kk_ref, ka_ref, ones_ref, of_ref, ob_ref, sf_ref, sb_ref):
    @pl.when(pl.program_id(1) == 0)
    def _():
        sf_ref[...] = jnp.zeros_like(sf_ref)
        sb_ref[...] = jnp.zeros_like(sb_ref)

    kk_scale, k_a, ones_bd = kk_ref[...], ka_ref[...], ones_ref[...]
    of_ref[...] = _rwkv_dir(rf_ref[...], kf_ref[...], vf_ref[...], lwf_ref[...], af_ref[...], kk_scale, k_a, ones_bd,
                            sf_ref, False)
    ob_ref[...] = _rwkv_dir(rb_ref[...], kb_ref[...], vb_ref[...], lwb_ref[...], ab_ref[...], kk_scale, k_a, ones_bd,
                            sb_ref, True)


def _rwkv_scan(rkv, lwa, kk_scale, k_a, batch, seq):
    c = RWKV_CHUNK
    nc = seq // c
    w = RWKV_WIDTH
    gw = RWKV_GROUP_HEADS * RWKV_N
    head = np.arange(w) // RWKV_N
    ones_bd = jnp.asarray(head[:, None] == head[None, :], BF16)

    def tok(col, rev):
        return pl.BlockSpec((None, c, w), lambda b, ci: (b, nc - 1 - ci if rev else ci, col))

    vec = pl.BlockSpec((1, w), lambda b, ci: (0, 0))
    shape = jax.ShapeDtypeStruct((batch, seq, w), F32)
    state = pltpu.VMEM((RWKV_HEADS // RWKV_GROUP_HEADS, gw, gw), F32)
    return pl.pallas_call(
        _rwkv_chunk_kernel,
        grid=(batch, nc),
        in_specs=[
            tok(0, False), tok(1, False), tok(2, False), tok(0, False), tok(2, False),
            tok(0, True), tok(1, True), tok(2, True), tok(1, True), tok(3, True),
            vec, vec, pl.BlockSpec((w, w), lambda b, ci: (0, 0)),
        ],
        out_specs=[tok(0, False), tok(0, True)],
        out_shape=[shape, shape],
        scratch_shapes=[state, state],
        compiler_params=_params("parallel", "arbitrary"),
        name="rwkv_scan",
    )(rkv, rkv, rkv, lwa, lwa, rkv, rkv, rkv, lwa, lwa, kk_scale, k_a, ones_bd)


def _rwkv_post_kernel(yf_ref, yb_ref, r_ref, k_ref, v_ref, a0_ref, a1_ref, g_ref, ka_ref, rk_ref, lnw_ref, lnb_ref, o_ref):
    n = RWKV_N
    y = yf_ref[...] + yb_ref[...]
    k, v, ka = k_ref[...], v_ref[...], ka_ref[...]
    kd = k * (1.0 + (a0_ref[...] - 1.0) * ka) + k * (1.0 + (a1_ref[...] - 1.0) * ka)
    t = r_ref[...] * kd * rk_ref[...]
    lnw, lnb = lnw_ref[...], lnb_ref[...]
    outs = []
    for h in range(RWKV_HEADS):
        sl = slice(h * n, (h + 1) * n)
        yh = y[:, sl]
        mu = jnp.mean(yh, axis=-1, keepdims=True)
        yc = yh - mu
        var = jnp.mean(yc * yc, axis=-1, keepdims=True)
        bonus = jnp.sum(t[:, sl], axis=-1, keepdims=True)
        outs.append(yc * lax.rsqrt(var + RWKV_NORM_EPS) * lnw[:, sl] + lnb[:, sl] + bonus * v[:, sl])
    o_ref[...] = (jnp.concatenate(outs, axis=1) * g_ref[...]).astype(o_ref.dtype)


def _rwkv_post(yf, yb, rkv, lwa, g, k_a, r_k, ln_w, ln_b):
    batch, seq, w = yf.shape
    m = batch * seq
    tm = ROW_TILE
    rkv2 = rkv.reshape(m, 3 * w)
    lwa2 = lwa.reshape(m, 4 * w)
    tok = lambda col: pl.BlockSpec((tm, w), lambda i: (i, col))
    vec = pl.BlockSpec((1, w), lambda i: (0, 0))
    return pl.pallas_call(
        _rwkv_post_kernel,
        grid=(m // tm,),
        in_specs=[tok(0), tok(0), tok(0), tok(1), tok(2), tok(2), tok(3), tok(0), vec, vec, vec, vec],
        out_specs=pl.BlockSpec((tm, w), lambda i: (i, 0)),
        out_shape=jax.ShapeDtypeStruct((m, w), BF16),
        compiler_params=_params("parallel"),
        name="rwkv_post",
    )(yf.reshape(m, w), yb.reshape(m, w), rkv2, rkv2, rkv2, lwa2, lwa2, g.reshape(m, w), k_a, r_k, ln_w, ln_b)


def _merge_kernel(ya_ref, yb_ref, yc_ref, ga_ref, gb_ref, gc_ref, wa_ref, wb_ref, wc_ref, o_ref):
    acc = _sigmoid(ga_ref[...]) * _dot(ya_ref[...], wa_ref[...])
    acc += _sigmoid(gb_ref[...]) * _dot(yb_ref[...], wb_ref[...])
    acc += _sigmoid(gc_ref[...]) * _dot(yc_ref[...], wc_ref[...])
    o_ref[...] = acc.astype(o_ref.dtype)


def _merge(ya, yb, yc, gates, wa, wb, wc):
    m = ya.shape[0]
    d = wa.shape[1]
    tm, tn = ROW_TILE, 512
    nj = d // tn
    act = lambda a: pl.BlockSpec((tm, a.shape[1]), lambda i, j: (i, 0))
    wsp = lambda a: pl.BlockSpec((a.shape[0], tn), lambda i, j: (0, j))
    gsp = lambda b: pl.BlockSpec((tm, tn), lambda i, j: (i, b * nj + j))
    return pl.pallas_call(
        _merge_kernel,
        grid=(m // tm, nj),
        in_specs=[act(ya), act(yb), act(yc), gsp(0), gsp(1), gsp(2), wsp(wa), wsp(wb), wsp(wc)],
        out_specs=pl.BlockSpec((tm, tn), lambda i, j: (i, j)),
        out_shape=jax.ShapeDtypeStruct((m, d), BF16),
        compiler_params=_params("parallel", "arbitrary"),
        name="branch_merge",
    )(ya, yb, yc, gates, gates, gates, wa, wb, wc)


def _out_proj_kernel(x_ref, m_ref, w_ref, o_ref):
    o_ref[...] = x_ref[...] + _dot(m_ref[...], w_ref[...])


def _out_proj(x, merged, w):
    m, d = x.shape
    tm, tn = MM_ROW_TILE, 512
    return pl.pallas_call(
        _out_proj_kernel,
        grid=(m // tm, d // tn),
        in_specs=[
            pl.BlockSpec((tm, tn), lambda i, j: (i, j)),
            pl.BlockSpec((tm, d), lambda i, j: (i, 0)),
            pl.BlockSpec((d, tn), lambda i, j: (0, j)),
        ],
        out_specs=pl.BlockSpec((tm, tn), lambda i, j: (i, j)),
        out_shape=jax.ShapeDtypeStruct((m, d), F32),
        compiler_params=_params("parallel", "arbitrary"),
        name="out_proj",
    )(x, merged, w)


def _block_diag2(w2):
    zero = jnp.zeros_like(w2[0])
    return jnp.concatenate([jnp.concatenate([w2[0], zero], axis=1), jnp.concatenate([zero, w2[1]], axis=1)], axis=0)


def _mixers(x, u, p, batch, seq):
    outs, lses = [], []
    for g in range(len(ATT_GROUPS)):
        qkv = _matmul(u, p["w_qkv"][g], 768, BF16)
        o, l = _attention_group(qkv, g, batch, seq)
        outs.append(o)
        lses.append(l)
    ya = _att_merge(outs, lses)

    rproj = _matmul(u, p["w_ret"], 1024, F32)
    ret_f, ret_b = _retention(rproj, p["ret_log_g"], batch, seq)
    yb = _ret_post(ret_f, ret_b, rproj, p["ret_norm"])

    cfeat = _matmul(u, p["w_rwkv"], 640, F32)
    rkv, lwa, g = _rwkv_prep(cfeat, p["rwkv_conv"], p["rwkv_w0"], p["rwkv_w2bd"], p["rwkv_a0"], p["rwkv_a2bd"],
                             p["rwkv_g2"], batch, seq)
    yf, yb_dir = _rwkv_scan(rkv, lwa, p["rwkv_k_k"], p["rwkv_k_a"], batch, seq)
    yc = _rwkv_post(yf, yb_dir, rkv, lwa, g, p["rwkv_k_a"], p["rwkv_r_k"], p["rwkv_ln_w"], p["rwkv_ln_b"])

    gates = _matmul(u, p["w_gates"], 1024, F32)
    merged = _merge(ya, yb, yc, gates, p["w_branch_a"], p["w_branch_b"], p["w_branch_c"])
    return _out_proj(x, merged, p["w_out"])


def kernel(x_prompt, x_sample, ffn1_norm, ffn1_w_gate, ffn1_w_up, ffn1_w_down, mix_norm, w_in, ret_decay_logit, ret_norm, rwkv_conv, rwkv_w0, rwkv_w2, rwkv_a0, rwkv_a2, rwkv_g2, rwkv_k_k, rwkv_k_a, rwkv_r_k, rwkv_ln_w, rwkv_ln_b, w_branch_a, w_branch_b, w_branch_c, w_out, ffn2_norm, ffn2_w_gate, ffn2_w_up, ffn2_w_down, final_norm):
    depth = ffn1_norm.shape[0]
    w = RWKV_WIDTH
    ng, aw = len(ATT_GROUPS), ATT_OUT_WIDTH
    att3 = 3 * ng * aw
    ret_w = 2 * RET_HEADS * RET_DK + 2 * RET_HEADS * RET_DV
    layers = []
    for l in range(depth):
        w_in_l = w_in[l].astype(BF16)
        o1, o2, o3 = att3, att3 + ret_w, att3 + ret_w + RWKV_CONV_CH
        w_qkv = [jnp.concatenate([w_in_l[:, (t * ng + g) * aw:(t * ng + g + 1) * aw] for t in range(3)], axis=1)
                 for g in range(ng)]
        layers.append({
            "ffn1": (ffn1_norm[l], ffn1_w_gate[l].astype(BF16), ffn1_w_up[l].astype(BF16), ffn1_w_down[l].astype(BF16)),
            "ffn2": (ffn2_norm[l], ffn2_w_gate[l].astype(BF16), ffn2_w_up[l].astype(BF16), ffn2_w_down[l].astype(BF16)),
            "mix_norm": mix_norm[l],
            "w_qkv": w_qkv, "w_ret": w_in_l[:, o1:o2], "w_rwkv": w_in_l[:, o2:o3], "w_gates": w_in_l[:, o3:],
            "ret_log_g": jax.nn.log_sigmoid(ret_decay_logit[l].astype(F32)),
            "ret_norm": ret_norm[l].astype(F32),
            "rwkv_conv": rwkv_conv[l].astype(F32),
            "rwkv_w0": rwkv_w0[l].astype(F32).reshape(1, 2 * w),
            "rwkv_w2bd": _block_diag2(rwkv_w2[l]).astype(BF16),
            "rwkv_a0": rwkv_a0[l].astype(F32).reshape(1, 2 * w),
            "rwkv_a2bd": _block_diag2(rwkv_a2[l]).astype(BF16),
            "rwkv_g2": rwkv_g2[l].astype(BF16),
            "rwkv_k_k": rwkv_k_k[l].astype(F32).reshape(1, w),
            "rwkv_k_a": rwkv_k_a[l].astype(F32).reshape(1, w),
            "rwkv_r_k": rwkv_r_k[l].astype(F32).reshape(1, w),
            "rwkv_ln_w": rwkv_ln_w[l].astype(F32).reshape(1, w),
            "rwkv_ln_b": rwkv_ln_b[l].astype(F32).reshape(1, w),
            "w_branch_a": w_branch_a[l].astype(BF16), "w_branch_b": w_branch_b[l].astype(BF16),
            "w_branch_c": w_branch_c[l].astype(BF16), "w_out": w_out[l].astype(BF16),
        })

    def run(x):
        batch, seq, d = x.shape
        y = x.reshape(batch * seq, d)
        for l, p in enumerate(layers):
            y, u = _ffn(y, *p["ffn1"], p["mix_norm"], "next_norm")
            y = _mixers(y, u, p, batch, seq)
            y, = _ffn(y, *p["ffn2"], final_norm, "final" if l == depth - 1 else "plain")
        return y.reshape(batch, seq, d)

    return (run(x_prompt), run(x_sample))
```

```python
import functools

import numpy as np
import jax
import jax.numpy as jnp
from jax import lax
from jax.experimental import pallas as pl
from jax.experimental.pallas import tpu as pltpu

F32 = jnp.float32
BF16 = jnp.bfloat16

NORM_EPS = 1e-6

ATT_GROUPS = ((128, 1), (512, 4), (2048, 16))
ATT_HEADS = 4
ATT_HEAD_DIM = 128
ATT_OUT_WIDTH = ATT_HEADS * ATT_HEAD_DIM
ATT_HALF = 64
ATT_TQ = 128
ATT_BLOCK = 512
ATT_LSE_LANES = 128 // ATT_HEADS

RET_HEADS = 4
RET_DK = 128
RET_DV = 256
RET_CHUNK = 128
RET_NORM_EPS = 1e-5

RWKV_HEADS = 8
RWKV_N = 64
RWKV_WIDTH = RWKV_HEADS * RWKV_N
RWKV_CONV_CH = 1920
RWKV_NORM_EPS = 64e-5
RWKV_CHUNK = 64
RWKV_GROUP_HEADS = 4
RWKV_BATCH_ROWS = 2

ROW_TILE = 512
MM_ROW_TILE = 1024
VMEM_LIMIT_BYTES = 56 * 1024 * 1024
NEG_BIG = -1e30


def _params(*sem):
    return pltpu.CompilerParams(dimension_semantics=sem, vmem_limit_bytes=VMEM_LIMIT_BYTES)


def _dot(a, b):
    return jnp.dot(a, b, preferred_element_type=F32)


def _dot_nt(a, b):
    return lax.dot_general(a, b, (((1,), (1,)), ((), ())), preferred_element_type=F32)


def _dot_tn(a, b):
    return lax.dot_general(a, b, (((0,), (0,)), ((), ())), preferred_element_type=F32)


def _split2(a):
    hi = a.astype(BF16)
    lo = (a - hi.astype(F32)).astype(BF16)
    return hi, lo


def _rms(x, gain):
    ms = jnp.mean(x * x, axis=-1, keepdims=True)
    return x * lax.rsqrt(ms + NORM_EPS) * gain


def _sigmoid(x):
    return 1.0 / (1.0 + jnp.exp(-x))


def _softplus(x):
    return jnp.maximum(x, 0.0) + jnp.log(1.0 + jnp.exp(-jnp.abs(x)))


def _ffn_kernel(x_ref, g_ref, wg_ref, wu_ref, wd_ref, g2_ref, o_ref, *rest, nf, mode):
    xn_ref = rest[-1]
    f = pl.program_id(1)

    @pl.when(f == 0)
    def _():
        xn_ref[...] = _rms(x_ref[...], g_ref[...]).astype(BF16)
        o_ref[...] = jnp.zeros_like(o_ref)

    xn = xn_ref[...]
    hg = _dot(xn, wg_ref[...])
    hu = _dot(xn, wu_ref[...])
    h = (hg * _sigmoid(hg) * hu).astype(BF16)
    o_ref[...] += _dot(h, wd_ref[...])

    @pl.when(f == nf - 1)
    def _():
        y = x_ref[...] + 0.5 * o_ref[...]
        if mode == "final":
            y = _rms(y, g2_ref[...])
        o_ref[...] = y
        if mode == "next_norm":
            rest[0][...] = _rms(y, g2_ref[...]).astype(BF16)


def _ffn(x, gain, wg, wu, wd, gain2, mode):
    m, d = x.shape
    ff = wg.shape[1]
    tm, tf = ROW_TILE, 512
    nf = ff // tf
    row = pl.BlockSpec((tm, d), lambda i, f: (i, 0))
    vec = pl.BlockSpec((1, d), lambda i, f: (0, 0))
    out_specs, out_shape = [row], [jax.ShapeDtypeStruct((m, d), F32)]
    if mode == "next_norm":
        out_specs.append(row)
        out_shape.append(jax.ShapeDtypeStruct((m, d), BF16))
    return pl.pallas_call(
        functools.partial(_ffn_kernel, nf=nf, mode=mode),
        grid=(m // tm, nf),
        in_specs=[row, vec,
                  pl.BlockSpec((d, tf), lambda i, f: (0, f)),
                  pl.BlockSpec((d, tf), lambda i, f: (0, f)),
                  pl.BlockSpec((tf, d), lambda i, f: (f, 0)),
                  vec],
        out_specs=out_specs,
        out_shape=out_shape,
        scratch_shapes=[pltpu.VMEM((tm, d), BF16)],
        compiler_params=_params("parallel", "arbitrary"),
        name="ffn",
    )(x, gain.reshape(1, d), wg, wu, wd, gain2.reshape(1, d))


def _matmul_kernel(x_ref, w_ref, o_ref):
    o_ref[...] = _dot(x_ref[...], w_ref[...]).astype(o_ref.dtype)


def _matmul(x, w, tn, out_dtype):
    m, d = x.shape
    n = w.shape[1]
    tm = MM_ROW_TILE
    return pl.pallas_call(
        _matmul_kernel,
        grid=(m // tm, n // tn),
        in_specs=[pl.BlockSpec((tm, d), lambda i, j: (i, 0)), pl.BlockSpec((d, tn), lambda i, j: (0, j))],
        out_specs=pl.BlockSpec((tm, tn), lambda i, j: (i, j)),
        out_shape=jax.ShapeDtypeStruct((m, n), out_dtype),
        compiler_params=_params("parallel", "arbitrary"),
        name="proj",
    )(x, w)


def _attn_kernel(q_ref, kp_ref, kc_ref, kn_ref, vp_ref, vc_ref, vn_ref, o_ref, l_ref, *, dil, slopes, sub_len):
    tq, half, dh, ll = ATT_TQ, ATT_HALF, ATT_HEAD_DIM, ATT_LSE_LANES
    tqb = q_ref.shape[0]
    i = pl.program_id(2)
    qi = lax.broadcasted_iota(jnp.int32, (tq, tq + 2 * half), 0)
    kj = lax.broadcasted_iota(jnp.int32, (tq, tq + 2 * half), 1)
    rel = kj - half - qi
    band = jnp.where(jnp.abs(rel) <= half, 1, 0)
    dist = (dil * jnp.abs(rel)).astype(F32)
    scale = dh ** -0.5
    oks = []
    for j in range(tqb // tq):
        kidx = i * tqb + j * tq - half + kj
        oks.append(band * jnp.where(kidx >= 0, 1, 0) * jnp.where(kidx < sub_len, 1, 0) > 0)
    for h in range(ATT_HEADS):
        sl = slice(h * dh, (h + 1) * dh)
        kfull = jnp.concatenate([kp_ref[:, sl], kc_ref[:, sl], kn_ref[:, sl]], axis=0)
        vfull = jnp.concatenate([vp_ref[:, sl], vc_ref[:, sl], vn_ref[:, sl]], axis=0)
        nsub = tqb // tq
        rows = [slice(j * tq, (j + 1) * tq) for j in range(nsub)]
        keys = [slice(j * tq, (j + 1) * tq + 2 * half) for j in range(nsub)]
        scores = [_dot_nt(q_ref[rows[j], sl], kfull[keys[j]]) for j in range(nsub)]
        probs, dens = [], []
        for j in range(nsub):
            s = jnp.where(oks[j], scores[j] * scale - slopes[h] * dist, NEG_BIG)
            mx = jnp.max(s, axis=-1, keepdims=True)
            p = jnp.exp(s - mx)
            den = jnp.sum(p, axis=-1, keepdims=True)
            l_ref[rows[j], h * ll:(h + 1) * ll] = jnp.broadcast_to(mx + jnp.log(den), (tq, ll))
            probs.append(p.astype(BF16))
            dens.append(den)
        for j in range(nsub):
            o_ref[rows[j], sl] = _dot(probs[j], vfull[keys[j]]) / dens[j]


def _attention_group(qkv, g, batch, seq):
    dil = ATT_GROUPS[g][1]
    sub_len = seq // dil
    tqb = min(ATT_BLOCK, sub_len)
    nq = sub_len // tqb
    half = ATT_HALF
    w = ATT_OUT_WIDTH
    n = len(ATT_GROUPS) * ATT_HEADS
    slopes_all = np.power(np.float32(2.0), -8.0 * np.arange(1, n + 1, dtype=np.float32) / n)
    slopes = tuple(float(s) for s in slopes_all[g * ATT_HEADS:(g + 1) * ATT_HEADS])
    x = qkv.reshape(batch, sub_len, dil * 3 * w)
    per, last = tqb // half, sub_len // half - 1

    def spec(col, where):
        if where == 0:
            return pl.BlockSpec((None, tqb, w), lambda b, p, i: (b, i, p * 3 + col))
        if where < 0:
            return pl.BlockSpec((None, half, w), lambda b, p, i: (b, jnp.maximum(i * per - 1, 0), p * 3 + col))
        return pl.BlockSpec((None, half, w), lambda b, p, i: (b, jnp.minimum((i + 1) * per, last), p * 3 + col))

    o, lse = pl.pallas_call(
        functools.partial(_attn_kernel, dil=dil, slopes=slopes, sub_len=sub_len),
        grid=(batch, dil, nq),
        in_specs=[spec(0, 0), spec(1, -1), spec(1, 0), spec(1, 1), spec(2, -1), spec(2, 0), spec(2, 1)],
        out_specs=[pl.BlockSpec((None, tqb, w), lambda b, p, i: (b, i, p)),
                   pl.BlockSpec((None, tqb, 128), lambda b, p, i: (b, i, p))],
        out_shape=[jax.ShapeDtypeStruct((batch, sub_len, dil * w), F32),
                   jax.ShapeDtypeStruct((batch, sub_len, dil * 128), F32)],
        compiler_params=_params("parallel", "parallel", "arbitrary"),
        name=f"dilated_attn_g{g}",
    )(x, x, x, x, x, x, x)
    return o.reshape(batch * seq, w), lse.reshape(batch * seq, 128)


def _att_merge_kernel(o0, o1, o2, l0, l1, l2, y_ref):
    a, b, c = l0[...], l1[...], l2[...]
    mx = jnp.maximum(jnp.maximum(a, b), c)
    ea, eb, ec = jnp.exp(a - mx), jnp.exp(b - mx), jnp.exp(c - mx)
    inv = 1.0 / (ea + eb + ec)
    wa, wb, wc = ea * inv, eb * inv, ec * inv
    dh, ll = ATT_HEAD_DIM, ATT_LSE_LANES
    tm = a.shape[0]
    for h in range(ATT_HEADS):
        sl = slice(h * dh, (h + 1) * dh)
        bc = lambda t: jnp.broadcast_to(t[:, h * ll:h * ll + 1], (tm, dh))
        y_ref[:, sl] = (bc(wa) * o0[:, sl] + bc(wb) * o1[:, sl] + bc(wc) * o2[:, sl]).astype(y_ref.dtype)


def _att_merge(outs, lses):
    m, w = outs[0].shape
    tm = ROW_TILE
    spec = pl.BlockSpec((tm, w), lambda i: (i, 0))
    lspec = pl.BlockSpec((tm, 128), lambda i: (i, 0))
    return pl.pallas_call(
        _att_merge_kernel,
        grid=(m // tm,),
        in_specs=[spec] * 3 + [lspec] * 3,
        out_specs=spec,
        out_shape=jax.ShapeDtypeStruct((m, w), BF16),
        compiler_params=_params("parallel"),
        name="att_merge",
    )(*outs, *lses)


def _ret_kernel(lg_ref, qf_ref, kf_ref, vf_ref, qb_ref, kb_ref, vb_ref, of_ref, ob_ref, sf_ref, sb_ref):
    c, dk, dv = RET_CHUNK, RET_DK, RET_DV

    @pl.when(pl.program_id(1) == 0)
    def _():
        sf_ref[...] = jnp.zeros_like(sf_ref)
        sb_ref[...] = jnp.zeros_like(sb_ref)

    pos = lax.broadcasted_iota(jnp.int32, (c, 1), 0).astype(F32)
    diff = (lax.broadcasted_iota(jnp.int32, (c, c), 0) - lax.broadcasted_iota(jnp.int32, (c, c), 1)).astype(F32)
    up = jnp.maximum(diff, 0.0)
    dn = jnp.maximum(-diff, 0.0)
    for h in range(RET_HEADS):
        ks, vs = slice(h * dk, (h + 1) * dk), slice(h * dv, (h + 1) * dv)
        lgf = jnp.full((1, 1), lg_ref[0, h], F32)
        lgb = jnp.full((1, 1), lg_ref[1, h], F32)

        q = qf_ref[:, ks].astype(BF16)
        k = kf_ref[:, ks] * (dk ** -0.5)
        v = vf_ref[:, vs].astype(BF16)
        state = sf_ref[h]
        dmat = jnp.where(diff >= 0, jnp.exp(up * lgf), 0.0) + jnp.where(diff <= 0, jnp.exp(dn * lgb), 0.0)
        scores = _dot_nt(q, k.astype(BF16)) * dmat
        of_ref[:, vs] = _dot(scores.astype(BF16), v) + _dot(q, state.astype(BF16)) * jnp.exp((pos + 1.0) * lgf)
        kz = (k * jnp.exp((c - 1.0 - pos) * lgf)).astype(BF16)
        sf_ref[h] = jnp.exp(c * lgf) * state + _dot_tn(kz, v)

        q = qb_ref[:, ks].astype(BF16)
        k = kb_ref[:, ks] * (dk ** -0.5)
        v = vb_ref[:, vs].astype(BF16)
        state = sb_ref[h]
        ob_ref[:, vs] = _dot(q, state.astype(BF16)) * jnp.exp((c - pos) * lgb)
        kz = (k * jnp.exp(pos * lgb)).astype(BF16)
        sb_ref[h] = jnp.exp(c * lgb) * state + _dot_tn(kz, v)


def _retention(rproj, log_g, batch, seq):
    c = RET_CHUNK
    nc = seq // c
    qk_w, v_w = RET_HEADS * RET_DK, RET_HEADS * RET_DV
    x = rproj.reshape(batch, seq, rproj.shape[1])

    def tok(width, col, rev):
        return pl.BlockSpec((None, c, width), lambda b, ci: (b, nc - 1 - ci if rev else ci, col))

    shape = jax.ShapeDtypeStruct((batch, seq, v_w), F32)
    state = pltpu.VMEM((RET_HEADS, RET_DK, RET_DV), F32)
    v_col = 2 * qk_w // v_w
    return pl.pallas_call(
        _ret_kernel,
        grid=(batch, nc),
        in_specs=[pl.BlockSpec(memory_space=pltpu.SMEM),
                  tok(qk_w, 0, False), tok(qk_w, 1, False), tok(v_w, v_col, False),
                  tok(qk_w, 0, True), tok(qk_w, 1, True), tok(v_w, v_col, True)],
        out_specs=[tok(v_w, 0, False), tok(v_w, 0, True)],
        out_shape=[shape, shape],
        scratch_shapes=[state, state],
        compiler_params=_params("parallel", "arbitrary"),
        name="retention",
    )(log_g, x, x, x, x, x, x)


def _ret_post_kernel(rf_ref, rb_ref, g_ref, gain_ref, y_ref):
    ret = rf_ref[...] + rb_ref[...]
    gate = g_ref[...]
    gate = gate * _sigmoid(gate)
    gain = gain_ref[...]
    for h in range(RET_HEADS):
        sl = slice(h * RET_DV, (h + 1) * RET_DV)
        y = ret[:, sl]
        mu = jnp.mean(y, axis=-1, keepdims=True)
        yc = y - mu
        var = jnp.mean(yc * yc, axis=-1, keepdims=True)
        y_ref[:, sl] = (gate[:, sl] * yc * lax.rsqrt(var + RET_NORM_EPS) * gain[:, sl]).astype(y_ref.dtype)


def _ret_post(ret_f, ret_b, rproj, gain):
    batch, seq, w = ret_f.shape
    m = batch * seq
    tm = ROW_TILE
    gcol = rproj.shape[1] // w - 1
    row = pl.BlockSpec((tm, w), lambda i: (i, 0))
    return pl.pallas_call(
        _ret_post_kernel,
        grid=(m // tm,),
        in_specs=[row, row, pl.BlockSpec((tm, w), lambda i: (i, gcol)), pl.BlockSpec((1, w), lambda i: (0, 0))],
        out_specs=row,
        out_shape=jax.ShapeDtypeStruct((m, w), BF16),
        compiler_params=_params("parallel"),
        name="ret_post",
    )(ret_f.reshape(m, w), ret_b.reshape(m, w), rproj, gain.reshape(1, w))


def _rwkv_prep_kernel(x_ref, xp_ref, xn_ref, conv_ref, w0_ref, w2_ref, a0_ref, a2_ref, g2_ref,
                      rkv_ref, lwa_ref, g_ref, *, ns):
    i = pl.program_id(1)
    x = x_ref[...]
    ts = x.shape[0]
    w = RWKV_WIDTH
    row = lax.broadcasted_iota(jnp.int32, (ts, 1), 0)
    prev_row = xp_ref[7:8, :] * jnp.where(i > 0, 1.0, 0.0)
    next_row = xn_ref[0:1, :] * jnp.where(i < ns - 1, 1.0, 0.0)
    x_m1 = jnp.where(row == 0, prev_row, pltpu.roll(x, 1, 0))
    x_p1 = jnp.where(row == ts - 1, next_row, pltpu.roll(x, ts - 1, 0))
    cw = conv_ref[...]
    feats = cw[0:1, :] * x_m1 + cw[1:2, :] * x + cw[2:3, :] * x_p1
    rkv_ref[...] = feats[:, :3 * w]
    w_lo = jnp.tanh(feats[:, 3 * w:3 * w + 128]).astype(BF16)
    a_lo = feats[:, 3 * w + 128:3 * w + 256].astype(BF16)
    g_lo = _sigmoid(feats[:, 3 * w + 256:3 * w + 384]).astype(BF16)
    logw = -_softplus(-(w0_ref[...] + _dot(w_lo, w2_ref[...]))) - 0.5
    lwa_ref[:, :2 * w] = -jnp.exp(logw)
    lwa_ref[:, 2 * w:] = _sigmoid(a0_ref[...] + _dot(a_lo, a2_ref[...]))
    g_ref[...] = _dot(g_lo, g2_ref[...])


def _rwkv_prep(cfeat, conv, w0, w2bd, a0, a2bd, g2, batch, seq):
    ts = 512
    ns = seq // ts
    ch = cfeat.shape[1]
    w = RWKV_WIDTH
    x = cfeat.reshape(batch, seq, ch)
    full = lambda shape: pl.BlockSpec(shape, lambda b, i: (0, 0))
    return pl.pallas_call(
        functools.partial(_rwkv_prep_kernel, ns=ns),
        grid=(batch, ns),
        in_specs=[
            pl.BlockSpec((None, ts, ch), lambda b, i: (b, i, 0)),
            pl.BlockSpec((None, 8, ch), lambda b, i: (b, jnp.maximum(i * (ts // 8) - 1, 0), 0)),
            pl.BlockSpec((None, 8, ch), lambda b, i: (b, jnp.minimum((i + 1) * (ts // 8), seq // 8 - 1), 0)),
            full((3, ch)), full((1, 2 * w)), full((128, 2 * w)), full((1, 2 * w)), full((128, 2 * w)), full((128, w)),
        ],
        out_specs=[
            pl.BlockSpec((None, ts, 3 * w), lambda b, i: (b, i, 0)),
            pl.BlockSpec((None, ts, 4 * w), lambda b, i: (b, i, 0)),
            pl.BlockSpec((None, ts, w), lambda b, i: (b, i, 0)),
        ],
        out_shape=[
            jax.ShapeDtypeStruct((batch, seq, 3 * w), F32),
            jax.ShapeDtypeStruct((batch, seq, 4 * w), F32),
            jax.ShapeDtypeStruct((batch, seq, w), F32),
        ],
        compiler_params=_params("parallel", "arbitrary"),
        name="rwkv_prep",
    )(x, x, x, conv, w0, w2bd, a0, a2bd, g2)


def _rwkv_fold_decay(r, k, v, lw, a, kk_scale, k_a, ones_bd, reverse):
    c = RWKV_CHUNK
    sgn = -1 if reverse else 1
    d = (lax.broadcasted_iota(jnp.int32, (c, c), 0) - lax.broadcasted_iota(jnp.int32, (c, c), 1)) * sgn
    tri = jnp.where(d >= 0, 1.0, 0.0).astype(BF16)
    hi = lw.astype(BF16)
    rem = lw - hi.astype(F32)
    mid = rem.astype(BF16)
    lo = (rem - mid.astype(F32)).astype(BF16)
    cs = _dot(tri, hi) + _dot(tri, mid) + _dot(tri, lo)
    g_all = jnp.exp(jnp.sum(lw, axis=0, keepdims=True))
    e_neg = jnp.exp(-cs)
    kks = k * kk_scale
    sq_hi, sq_lo = _split2(kks * kks)
    kk = kks * lax.rsqrt(_dot(sq_hi, ones_bd) + _dot(sq_lo, ones_bd) + 1e-12)
    abar_all = -kk * jnp.exp(cs - lw)
    bbar_all = kk * a * e_neg
    kbar_all = k * (1.0 + (a - 1.0) * k_a) * e_neg
    rbar_all = r * jnp.exp(cs)
    return abar_all, rbar_all, bbar_all, kbar_all, v, g_all


def _rwkv_masks(reverse):
    n, gw = RWKV_N, RWKV_GROUP_HEADS * RWKV_N
    row = lax.broadcasted_iota(jnp.int32, (gw, gw), 0)
    col = lax.broadcasted_iota(jnp.int32, (gw, gw), 1)
    same_head = (row // n) == (col // n)
    dd = jnp.where(same_head, (row - col) * (-1 if reverse else 1), -1)
    return same_head, dd > 0, dd >= 0, jnp.where(row == col, 1.0, 0.0)


def _rwkv_group_chain(folded, masks, g, s_ref):
    c, n, gh = RWKV_CHUNK, RWKV_N, RWKV_GROUP_HEADS
    gw = gh * n
    abar_all, rbar_all, bbar_all, kbar_all, v, g_all = folded
    same_head, strict, incl, eye = masks
    tile = lambda x: jnp.concatenate([x] * gh, axis=0)
    n_sq = c.bit_length() - 1
    gl = slice(g * gw, (g + 1) * gw)
    a_s = jnp.where(same_head, tile(abar_all[:, gl]), 0.0)
    r_s = jnp.where(same_head, tile(rbar_all[:, gl]), 0.0)
    v_s = jnp.where(same_head, tile(v[:, gl]), 0.0).astype(BF16)
    b_t = tile(bbar_all[:, gl].astype(BF16))
    k_t = tile(kbar_all[:, gl].astype(BF16))
    bk = jnp.concatenate([b_t, k_t], axis=0)
    f = _dot_nt(jnp.concatenate([a_s, r_s], axis=0).astype(BF16), bk)
    yield
    l_b = jnp.where(strict, f[:gw, :gw], 0.0)
    l_k = jnp.where(strict, f[:gw, gw:], 0.0).astype(BF16)
    m_b = jnp.where(incl, f[gw:, :gw], 0.0).astype(BF16)
    m_k = jnp.where(incl, f[gw:, gw:], 0.0).astype(BF16)
    tt = eye + l_b
    pwb = l_b.astype(BF16)
    pwb = _dot(pwb, pwb).astype(BF16)
    yield
    lkv = _dot(l_k, v_s).astype(BF16)
    yield
    for _ in range(n_sq - 2):
        both = _dot(pwb, jnp.concatenate([pwb, tt.astype(BF16)], axis=1))
        yield
        tt = tt + both[:, gw:]
        pwb = both[:, :gw].astype(BF16)
    tt = tt + _dot(pwb, tt.astype(BF16))
    yield
    zzb = _dot(tt.astype(BF16), jnp.concatenate([a_s.astype(BF16), lkv], axis=1)).astype(BF16)
    yield
    w_b = zzb[:, :gw]
    uv = jnp.concatenate([zzb[:, gw:], v_s], axis=0)
    qt = r_s + _dot(m_b, w_b)
    yield
    yloc = _dot(jnp.concatenate([m_b, m_k], axis=1), uv)
    yield
    state = s_ref[g]
    s_b = state.astype(BF16)
    y_s = _dot_nt(qt.astype(BF16), s_b) + yloc
    yield
    y = y_s[0:c]
    for i in range(1, gh):
        y = y + y_s[i * c:(i + 1) * c]
    gdec = g_all[:, gl]
    zb = _dot_tn(zzb, b_t)
    yield
    gmat = jnp.where(same_head, zb[gw:] + _dot_tn(v_s, k_t), 0.0) * gdec
    yield
    p_off = jnp.where(same_head, zb[:gw], 0.0) * gdec
    s_ref[g] = state * gdec + _dot(s_b, p_off.astype(BF16)) + gmat
    return y


def _rwkv_chunk_kernel(rf_ref, kf_ref, vf_ref, lwf_ref, af_ref, rb_ref, kb_ref, vb_ref, lwb_ref, ab_ref,
                       kk_ref, ka_ref, ones_ref, of_ref, ob_ref, sf_ref, sb_ref):
    @pl.when(pl.program_id(1) == 0)
    def _():
        sf_ref[...] = jnp.zeros_like(sf_ref)
        sb_ref[...] = jnp.zeros_like(sb_ref)

    kk_scale, k_a, ones_bd = kk_ref[...], ka_ref[...], ones_ref[...]
    masks = (_rwkv_masks(False), _rwkv_masks(True))
    ngroups = RWKV_HEADS // RWKV_GROUP_HEADS
    chains = []
    for b in range(rf_ref.shape[0]):
        fwd = _rwkv_fold_decay(rf_ref[b], kf_ref[b], vf_ref[b], lwf_ref[b], af_ref[b], kk_scale, k_a, ones_bd, False)
        bwd = _rwkv_fold_decay(rb_ref[b], kb_ref[b], vb_ref[b], lwb_ref[b], ab_ref[b], kk_scale, k_a, ones_bd, True)
        for g in range(ngroups):
            chains.append((of_ref, b, g, _rwkv_group_chain(fwd, masks[0], g, sf_ref.at[b])))
            chains.append((ob_ref, b, g, _rwkv_group_chain(bwd, masks[1], g, sb_ref.at[b])))
    gw = RWKV_GROUP_HEADS * RWKV_N
    while chains:
        live = []
        for o_ref, b, g, chain in chains:
            try:
                next(chain)
                live.append((o_ref, b, g, chain))
            except StopIteration as done:
                o_ref[b, :, g * gw:(g + 1) * gw] = done.value
        chains = live


def _rwkv_scan(rkv, lwa, kk_scale, k_a, batch, seq):
    c = RWKV_CHUNK
    nc = seq // c
    w = RWKV_WIDTH
    gw = RWKV_GROUP_HEADS * RWKV_N
    nb = RWKV_BATCH_ROWS
    head = np.arange(w) // RWKV_N
    ones_bd = jnp.asarray(head[:, None] == head[None, :], BF16)

    def tok(col, rev):
        return pl.BlockSpec((nb, c, w), lambda b, ci: (b, nc - 1 - ci if rev else ci, col))

    vec = pl.BlockSpec((1, w), lambda b, ci: (0, 0))
    shape = jax.ShapeDtypeStruct((batch, seq, w), F32)
    state = pltpu.VMEM((nb, RWKV_HEADS // RWKV_GROUP_HEADS, gw, gw), F32)
    return pl.pallas_call(
        _rwkv_chunk_kernel,
        grid=(batch // nb, nc),
        in_specs=[
            tok(0, False), tok(1, False), tok(2, False), tok(0, False), tok(2, False),
            tok(0, True), tok(1, True), tok(2, True), tok(1, True), tok(3, True),
            vec, vec, pl.BlockSpec((w, w), lambda b, ci: (0, 0)),
        ],
        out_specs=[tok(0, False), tok(0, True)],
        out_shape=[shape, shape],
        scratch_shapes=[state, state],
        compiler_params=_params("parallel", "arbitrary"),
        name="rwkv_scan",
    )(rkv, rkv, rkv, lwa, lwa, rkv, rkv, rkv, lwa, lwa, kk_scale, k_a, ones_bd)


def _rwkv_post_kernel(yf_ref, yb_ref, r_ref, k_ref, v_ref, a0_ref, a1_ref, g_ref, ka_ref, rk_ref, lnw_ref, lnb_ref, o_ref):
    n = RWKV_N
    y = yf_ref[...] + yb_ref[...]
    k, v, ka = k_ref[...], v_ref[...], ka_ref[...]
    kd = k * (1.0 + (a0_ref[...] - 1.0) * ka) + k * (1.0 + (a1_ref[...] - 1.0) * ka)
    t = r_ref[...] * kd * rk_ref[...]
    lnw, lnb = lnw_ref[...], lnb_ref[...]
    outs = []
    for h in range(RWKV_HEADS):
        sl = slice(h * n, (h + 1) * n)
        yh = y[:, sl]
        mu = jnp.mean(yh, axis=-1, keepdims=True)
        yc = yh - mu
        var = jnp.mean(yc * yc, axis=-1, keepdims=True)
        bonus = jnp.sum(t[:, sl], axis=-1, keepdims=True)
        outs.append(yc * lax.rsqrt(var + RWKV_NORM_EPS) * lnw[:, sl] + lnb[:, sl] + bonus * v[:, sl])
    o_ref[...] = (jnp.concatenate(outs, axis=1) * g_ref[...]).astype(o_ref.dtype)


def _rwkv_post(yf, yb, rkv, lwa, g, k_a, r_k, ln_w, ln_b):
    batch, seq, w = yf.shape
    m = batch * seq
    tm = ROW_TILE
    rkv2 = rkv.reshape(m, 3 * w)
    lwa2 = lwa.reshape(m, 4 * w)
    tok = lambda col: pl.BlockSpec((tm, w), lambda i: (i, col))
    vec = pl.BlockSpec((1, w), lambda i: (0, 0))
    return pl.pallas_call(
        _rwkv_post_kernel,
        grid=(m // tm,),
        in_specs=[tok(0), tok(0), tok(0), tok(1), tok(2), tok(2), tok(3), tok(0), vec, vec, vec, vec],
        out_specs=pl.BlockSpec((tm, w), lambda i: (i, 0)),
        out_shape=jax.ShapeDtypeStruct((m, w), BF16),
        compiler_params=_params("parallel"),
        name="rwkv_post",
    )(yf.reshape(m, w), yb.reshape(m, w), rkv2, rkv2, rkv2, lwa2, lwa2, g.reshape(m, w), k_a, r_k, ln_w, ln_b)


def _mix_out_kernel(x_ref, u_ref, ya_ref, yb_ref, yc_ref, ga_ref, gb_ref, gc_ref, wa_ref, wb_ref, wc_ref, wo_ref,
                    o_ref, m_ref, *, nj):
    j = pl.program_id(1)
    tn = o_ref.shape[1]

    @pl.when(j < nj)
    def _():
        u = u_ref[...]
        acc = _sigmoid(_dot(u, ga_ref[...])) * _dot(ya_ref[...], wa_ref[...])
        acc += _sigmoid(_dot(u, gb_ref[...])) * _dot(yb_ref[...], wb_ref[...])
        acc += _sigmoid(_dot(u, gc_ref[...])) * _dot(yc_ref[...], wc_ref[...])
        m_ref[j] = acc.astype(BF16)

    @pl.when(j >= nj)
    def _():
        acc = x_ref[...]
        for k in range(nj):
            acc += _dot(m_ref[k], wo_ref[k * tn:(k + 1) * tn, :])
        o_ref[...] = acc


def _mix_out(x, u, ya, yb, yc, w_gates, wa, wb, wc, w_out):
    m, d = x.shape
    tm, tn = ROW_TILE, 512
    nj = d // tn
    first = lambda j: jnp.minimum(j, nj - 1)
    second = lambda j: jnp.maximum(j - nj, 0)
    act = lambda a: pl.BlockSpec((tm, a.shape[1]), lambda i, j: (i, 0))
    wsp = lambda a: pl.BlockSpec((a.shape[0], tn), lambda i, j: (0, first(j)))
    gsp = lambda b: pl.BlockSpec((d, tn), lambda i, j: (0, b * nj + first(j)))
    return pl.pallas_call(
        functools.partial(_mix_out_kernel, nj=nj),
        grid=(m // tm, 2 * nj),
        in_specs=[pl.BlockSpec((tm, tn), lambda i, j: (i, second(j))),
                  act(u), act(ya), act(yb), act(yc), gsp(0), gsp(1), gsp(2), wsp(wa), wsp(wb), wsp(wc),
                  pl.BlockSpec((d, tn), lambda i, j: (0, second(j)))],
        out_specs=pl.BlockSpec((tm, tn), lambda i, j: (i, second(j))),
        out_shape=jax.ShapeDtypeStruct((m, d), F32),
        scratch_shapes=[pltpu.VMEM((nj, tm, tn), BF16)],
        compiler_params=_params("parallel", "arbitrary"),
        name="mix_out",
    )(x, u, ya, yb, yc, w_gates, w_gates, w_gates, wa, wb, wc, w_out)


def _block_diag2(w2):
    zero = jnp.zeros_like(w2[0])
    return jnp.concatenate([jnp.concatenate([w2[0], zero], axis=1), jnp.concatenate([zero, w2[1]], axis=1)], axis=0)


def _mixers(x, u, p, batch, seq):
    outs, lses = [], []
    for g in range(len(ATT_GROUPS)):
        qkv = _matmul(u, p["w_qkv"][g], 768, BF16)
        o, l = _attention_group(qkv, g, batch, seq)
        outs.append(o)
        lses.append(l)
    ya = _att_merge(outs, lses)

    rproj = _matmul(u, p["w_ret"], 1024, F32)
    ret_f, ret_b = _retention(rproj, p["ret_log_g"], batch, seq)
    yb = _ret_post(ret_f, ret_b, rproj, p["ret_norm"])

    cfeat = _matmul(u, p["w_rwkv"], 640, F32)
    rkv, lwa, g = _rwkv_prep(cfeat, p["rwkv_conv"], p["rwkv_w0"], p["rwkv_w2bd"], p["rwkv_a0"], p["rwkv_a2bd"],
                             p["rwkv_g2"], batch, seq)
    yf, yb_dir = _rwkv_scan(rkv, lwa, p["rwkv_k_k"], p["rwkv_k_a"], batch, seq)
    yc = _rwkv_post(yf, yb_dir, rkv, lwa, g, p["rwkv_k_a"], p["rwkv_r_k"], p["rwkv_ln_w"], p["rwkv_ln_b"])

    return _mix_out(x, u, ya, yb, yc, p["w_gates"], p["w_branch_a"], p["w_branch_b"], p["w_branch_c"], p["w_out"])


def kernel(x_prompt, x_sample, ffn1_norm, ffn1_w_gate, ffn1_w_up, ffn1_w_down, mix_norm, w_in, ret_decay_logit, ret_norm, rwkv_conv, rwkv_w0, rwkv_w2, rwkv_a0, rwkv_a2, rwkv_g2, rwkv_k_k, rwkv_k_a, rwkv_r_k, rwkv_ln_w, rwkv_ln_b, w_branch_a, w_branch_b, w_branch_c, w_out, ffn2_norm, ffn2_w_gate, ffn2_w_up, ffn2_w_down, final_norm):
    depth = ffn1_norm.shape[0]
    w = RWKV_WIDTH
    ng, aw = len(ATT_GROUPS), ATT_OUT_WIDTH
    att3 = 3 * ng * aw
    ret_w = 2 * RET_HEADS * RET_DK + 2 * RET_HEADS * RET_DV
    layers = []
    for l in range(depth):
        w_in_l = w_in[l].astype(BF16)
        o1, o2, o3 = att3, att3 + ret_w, att3 + ret_w + RWKV_CONV_CH
        w_qkv = [jnp.concatenate([w_in_l[:, (t * ng + g) * aw:(t * ng + g + 1) * aw] for t in range(3)], axis=1)
                 for g in range(ng)]
        layers.append({
            "ffn1": (ffn1_norm[l], ffn1_w_gate[l].astype(BF16), ffn1_w_up[l].astype(BF16), ffn1_w_down[l].astype(BF16)),
            "ffn2": (ffn2_norm[l], ffn2_w_gate[l].astype(BF16), ffn2_w_up[l].astype(BF16), ffn2_w_down[l].astype(BF16)),
            "mix_norm": mix_norm[l],
            "w_qkv": w_qkv, "w_ret": w_in_l[:, o1:o2], "w_rwkv": w_in_l[:, o2:o3], "w_gates": w_in_l[:, o3:],
            "ret_log_g": jax.nn.log_sigmoid(ret_decay_logit[l].astype(F32)),
            "ret_norm": ret_norm[l].astype(F32),
            "rwkv_conv": rwkv_conv[l].astype(F32),
            "rwkv_w0": rwkv_w0[l].astype(F32).reshape(1, 2 * w),
            "rwkv_w2bd": _block_diag2(rwkv_w2[l]).astype(BF16),
            "rwkv_a0": rwkv_a0[l].astype(F32).reshape(1, 2 * w),
            "rwkv_a2bd": _block_diag2(rwkv_a2[l]).astype(BF16),
            "rwkv_g2": rwkv_g2[l].astype(BF16),
            "rwkv_k_k": rwkv_k_k[l].astype(F32).reshape(1, w),
            "rwkv_k_a": rwkv_k_a[l].astype(F32).reshape(1, w),
            "rwkv_r_k": rwkv_r_k[l].astype(F32).reshape(1, w),
            "rwkv_ln_w": rwkv_ln_w[l].astype(F32).reshape(1, w),
            "rwkv_ln_b": rwkv_ln_b[l].astype(F32).reshape(1, w),
            "w_branch_a": w_branch_a[l].astype(BF16), "w_branch_b": w_branch_b[l].astype(BF16),
            "w_branch_c": w_branch_c[l].astype(BF16), "w_out": w_out[l].astype(BF16),
        })

    def run(x):
        batch, seq, d = x.shape
        y = x.reshape(batch * seq, d)
        for l, p in enumerate(layers):
            y, u = _ffn(y, *p["ffn1"], p["mix_norm"], "next_norm")
            y = _mixers(y, u, p, batch, seq)
            y, = _ffn(y, *p["ffn2"], final_norm, "final" if l == depth - 1 else "plain")
        return y.reshape(batch, seq, d)

    return (run(x_prompt), run(x_sample))
```

```python
import functools

import numpy as np
import jax
import jax.numpy as jnp
from jax import lax
from jax.experimental import pallas as pl
from jax.experimental.pallas import tpu as pltpu

F32 = jnp.float32
BF16 = jnp.bfloat16

NORM_EPS = 1e-6

ATT_GROUPS = ((128, 1), (512, 4), (2048, 16))
ATT_HEADS = 4
ATT_HEAD_DIM = 128
ATT_OUT_WIDTH = ATT_HEADS * ATT_HEAD_DIM
ATT_HALF = 64
ATT_TQ = 128
ATT_BLOCK = 512

RET_HEADS = 4
RET_DK = 128
RET_DV = 256
RET_CHUNK = 128
RET_NORM_EPS = 1e-5

RWKV_HEADS = 8
RWKV_N = 64
RWKV_WIDTH = RWKV_HEADS * RWKV_N
RWKV_CONV_CH = 1920
RWKV_NORM_EPS = 64e-5
RWKV_CHUNK = 64
RWKV_GROUP_HEADS = 4
RWKV_BATCH_ROWS = 2

ROW_TILE = 512
MM_ROW_TILE = 1024
VMEM_LIMIT_BYTES = 56 * 1024 * 1024
NEG_BIG = -1e30


def _params(*sem):
    return pltpu.CompilerParams(dimension_semantics=sem, vmem_limit_bytes=VMEM_LIMIT_BYTES)


def _dot(a, b):
    return jnp.dot(a, b, preferred_element_type=F32)


def _dot_nt(a, b):
    return lax.dot_general(a, b, (((1,), (1,)), ((), ())), preferred_element_type=F32)


def _dot_tn(a, b):
    return lax.dot_general(a, b, (((0,), (0,)), ((), ())), preferred_element_type=F32)


def _split2(a):
    hi = a.astype(BF16)
    lo = (a - hi.astype(F32)).astype(BF16)
    return hi, lo


def _rms(x, gain):
    ms = jnp.mean(x * x, axis=-1, keepdims=True)
    return x * lax.rsqrt(ms + NORM_EPS) * gain


def _sigmoid(x):
    return 1.0 / (1.0 + jnp.exp(-x))


def _softplus(x):
    return jnp.maximum(x, 0.0) + jnp.log(1.0 + jnp.exp(-jnp.abs(x)))


def _ffn_kernel(x_ref, g_ref, wg_ref, wu_ref, wd_ref, g2_ref, o_ref, *rest, nf, mode):
    xn_ref = rest[-1]
    f = pl.program_id(1)

    @pl.when(f == 0)
    def _():
        xn_ref[...] = _rms(x_ref[...], g_ref[...]).astype(BF16)
        o_ref[...] = jnp.zeros_like(o_ref)

    xn = xn_ref[...]
    hg = _dot(xn, wg_ref[...])
    hu = _dot(xn, wu_ref[...])
    h = (hg * _sigmoid(hg) * hu).astype(BF16)
    o_ref[...] += _dot(h, wd_ref[...])

    @pl.when(f == nf - 1)
    def _():
        y = x_ref[...] + 0.5 * o_ref[...]
        if mode == "final":
            y = _rms(y, g2_ref[...])
        o_ref[...] = y
        if mode == "next_norm":
            rest[0][...] = _rms(y, g2_ref[...]).astype(BF16)


def _ffn(x, gain, wg, wu, wd, gain2, mode):
    m, d = x.shape
    ff = wg.shape[1]
    tm, tf = ROW_TILE, 512
    nf = ff // tf
    row = pl.BlockSpec((tm, d), lambda i, f: (i, 0))
    vec = pl.BlockSpec((1, d), lambda i, f: (0, 0))
    out_specs, out_shape = [row], [jax.ShapeDtypeStruct((m, d), F32)]
    if mode == "next_norm":
        out_specs.append(row)
        out_shape.append(jax.ShapeDtypeStruct((m, d), BF16))
    return pl.pallas_call(
        functools.partial(_ffn_kernel, nf=nf, mode=mode),
        grid=(m // tm, nf),
        in_specs=[row, vec,
                  pl.BlockSpec((d, tf), lambda i, f: (0, f)),
                  pl.BlockSpec((d, tf), lambda i, f: (0, f)),
                  pl.BlockSpec((tf, d), lambda i, f: (f, 0)),
                  vec],
        out_specs=out_specs,
        out_shape=out_shape,
        scratch_shapes=[pltpu.VMEM((tm, d), BF16)],
        compiler_params=_params("parallel", "arbitrary"),
        name="ffn",
    )(x, gain.reshape(1, d), wg, wu, wd, gain2.reshape(1, d))


def _matmul_kernel(x_ref, w_ref, o_ref):
    o_ref[...] = _dot(x_ref[...], w_ref[...]).astype(o_ref.dtype)


def _matmul(x, w, tn, out_dtype):
    m, d = x.shape
    n = w.shape[1]
    tm = MM_ROW_TILE
    return pl.pallas_call(
        _matmul_kernel,
        grid=(m // tm, n // tn),
        in_specs=[pl.BlockSpec((tm, d), lambda i, j: (i, 0)), pl.BlockSpec((d, tn), lambda i, j: (0, j))],
        out_specs=pl.BlockSpec((tm, tn), lambda i, j: (i, j)),
        out_shape=jax.ShapeDtypeStruct((m, n), out_dtype),
        compiler_params=_params("parallel", "arbitrary"),
        name="proj",
    )(x, w)


def _attn_kernel(q_ref, kp_ref, kc_ref, kn_ref, vp_ref, vc_ref, vn_ref, o_ref, l_ref, *, dil, slopes, sub_len):
    tq, half, dh = ATT_TQ, ATT_HALF, ATT_HEAD_DIM
    tqb = q_ref.shape[0]
    i, phase = pl.program_id(1), pl.program_id(2)
    out_rows = lambda j: pl.ds(j * tq * dil + phase, tq, stride=dil) if dil > 1 else pl.ds(j * tq, tq)
    qi = lax.broadcasted_iota(jnp.int32, (tq, tq + 2 * half), 0)
    kj = lax.broadcasted_iota(jnp.int32, (tq, tq + 2 * half), 1)
    rel = kj - half - qi
    band = jnp.where(jnp.abs(rel) <= half, 1, 0)
    dist = (dil * jnp.abs(rel)).astype(F32)
    scale = dh ** -0.5
    oks = []
    for j in range(tqb // tq):
        kidx = i * tqb + j * tq - half + kj
        oks.append(band * jnp.where(kidx >= 0, 1, 0) * jnp.where(kidx < sub_len, 1, 0) > 0)
    for h in range(ATT_HEADS):
        sl = slice(h * dh, (h + 1) * dh)
        kfull = jnp.concatenate([kp_ref[:, sl], kc_ref[:, sl], kn_ref[:, sl]], axis=0)
        vfull = jnp.concatenate([vp_ref[:, sl], vc_ref[:, sl], vn_ref[:, sl]], axis=0)
        nsub = tqb // tq
        rows = [slice(j * tq, (j + 1) * tq) for j in range(nsub)]
        keys = [slice(j * tq, (j + 1) * tq + 2 * half) for j in range(nsub)]
        scores = [_dot_nt(q_ref[rows[j], sl], kfull[keys[j]]) for j in range(nsub)]
        probs, dens = [], []
        for j in range(nsub):
            s = jnp.where(oks[j], scores[j] * scale - slopes[h] * dist, NEG_BIG)
            mx = jnp.max(s, axis=-1, keepdims=True)
            p = jnp.exp(s - mx)
            den = jnp.sum(p, axis=-1, keepdims=True)
            l_ref[h, out_rows(j), :] = jnp.broadcast_to(mx + jnp.log(den), (tq, dh))
            probs.append(p.astype(BF16))
            dens.append(den)
        for j in range(nsub):
            o_ref[h, out_rows(j), :] = _dot(probs[j], vfull[keys[j]]) / dens[j]


def _attention_group(qkv, g, batch, seq):
    dil = ATT_GROUPS[g][1]
    sub_len = seq // dil
    tqb = min(ATT_BLOCK, sub_len)
    nq = sub_len // tqb
    half = ATT_HALF
    w = ATT_OUT_WIDTH
    n = len(ATT_GROUPS) * ATT_HEADS
    slopes_all = np.power(np.float32(2.0), -8.0 * np.arange(1, n + 1, dtype=np.float32) / n)
    slopes = tuple(float(s) for s in slopes_all[g * ATT_HEADS:(g + 1) * ATT_HEADS])
    x = qkv.reshape(batch, sub_len, dil * 3 * w)
    per, last = tqb // half, sub_len // half - 1

    def spec(col, where):
        if where == 0:
            return pl.BlockSpec((None, tqb, w), lambda b, i, p: (b, i, p * 3 + col))
        if where < 0:
            return pl.BlockSpec((None, half, w), lambda b, i, p: (b, jnp.maximum(i * per - 1, 0), p * 3 + col))
        return pl.BlockSpec((None, half, w), lambda b, i, p: (b, jnp.minimum((i + 1) * per, last), p * 3 + col))

    nh, dh = ATT_HEADS, ATT_HEAD_DIM
    out_spec = pl.BlockSpec((nh, None, tqb * dil, dh), lambda b, i, p: (0, b, i, 0))
    out_shape = jax.ShapeDtypeStruct((nh, batch, seq, dh), F32)
    o, lse = pl.pallas_call(
        functools.partial(_attn_kernel, dil=dil, slopes=slopes, sub_len=sub_len),
        grid=(batch, nq, dil),
        in_specs=[spec(0, 0), spec(1, -1), spec(1, 0), spec(1, 1), spec(2, -1), spec(2, 0), spec(2, 1)],
        out_specs=[out_spec, out_spec],
        out_shape=[out_shape, out_shape],
        compiler_params=_params("parallel", "parallel", "arbitrary"),
        name=f"dilated_attn_g{g}",
    )(x, x, x, x, x, x, x)
    return o.reshape(nh, batch * seq, dh), lse.reshape(nh, batch * seq, dh)


def _att_merge_kernel(o0, o1, o2, l0, l1, l2, y_ref):
    dh = ATT_HEAD_DIM
    for h in range(ATT_HEADS):
        a, b, c = l0[h], l1[h], l2[h]
        mx = jnp.maximum(jnp.maximum(a, b), c)
        ea, eb, ec = jnp.exp(a - mx), jnp.exp(b - mx), jnp.exp(c - mx)
        y = (ea * o0[h] + eb * o1[h] + ec * o2[h]) / (ea + eb + ec)
        y_ref[:, h * dh:(h + 1) * dh] = y.astype(y_ref.dtype)


def _att_merge(outs, lses):
    nh, m, dh = outs[0].shape
    tm = ROW_TILE
    spec = pl.BlockSpec((nh, tm, dh), lambda i: (0, i, 0))
    return pl.pallas_call(
        _att_merge_kernel,
        grid=(m // tm,),
        in_specs=[spec] * 6,
        out_specs=pl.BlockSpec((tm, nh * dh), lambda i: (i, 0)),
        out_shape=jax.ShapeDtypeStruct((m, nh * dh), BF16),
        compiler_params=_params("parallel"),
        name="att_merge",
    )(*outs, *lses)


def _ret_kernel(lg_ref, qf_ref, kf_ref, vf_ref, qb_ref, kb_ref, vb_ref, of_ref, ob_ref, sf_ref, sb_ref):
    c, dk, dv = RET_CHUNK, RET_DK, RET_DV

    @pl.when(pl.program_id(1) == 0)
    def _():
        sf_ref[...] = jnp.zeros_like(sf_ref)
        sb_ref[...] = jnp.zeros_like(sb_ref)

    pos = lax.broadcasted_iota(jnp.int32, (c, 1), 0).astype(F32)
    diff = (lax.broadcasted_iota(jnp.int32, (c, c), 0) - lax.broadcasted_iota(jnp.int32, (c, c), 1)).astype(F32)
    up = jnp.maximum(diff, 0.0)
    dn = jnp.maximum(-diff, 0.0)
    for h in range(RET_HEADS):
        ks, vs = slice(h * dk, (h + 1) * dk), slice(h * dv, (h + 1) * dv)
        lgf = jnp.full((1, 1), lg_ref[0, h], F32)
        lgb = jnp.full((1, 1), lg_ref[1, h], F32)

        q = qf_ref[:, ks].astype(BF16)
        k = kf_ref[:, ks] * (dk ** -0.5)
        v = vf_ref[:, vs].astype(BF16)
        state = sf_ref[h]
        dmat = jnp.where(diff >= 0, jnp.exp(up * lgf), 0.0) + jnp.where(diff <= 0, jnp.exp(dn * lgb), 0.0)
        scores = _dot_nt(q, k.astype(BF16)) * dmat
        of_ref[:, vs] = _dot(scores.astype(BF16), v) + _dot(q, state.astype(BF16)) * jnp.exp((pos + 1.0) * lgf)
        kz = (k * jnp.exp((c - 1.0 - pos) * lgf)).astype(BF16)
        sf_ref[h] = jnp.exp(c * lgf) * state + _dot_tn(kz, v)

        q = qb_ref[:, ks].astype(BF16)
        k = kb_ref[:, ks] * (dk ** -0.5)
        v = vb_ref[:, vs].astype(BF16)
        state = sb_ref[h]
        ob_ref[:, vs] = _dot(q, state.astype(BF16)) * jnp.exp((c - pos) * lgb)
        kz = (k * jnp.exp(pos * lgb)).astype(BF16)
        sb_ref[h] = jnp.exp(c * lgb) * state + _dot_tn(kz, v)


def _retention(rproj, log_g, batch, seq):
    c = RET_CHUNK
    nc = seq // c
    qk_w, v_w = RET_HEADS * RET_DK, RET_HEADS * RET_DV
    x = rproj.reshape(batch, seq, rproj.shape[1])

    def tok(width, col, rev):
        return pl.BlockSpec((None, c, width), lambda b, ci: (b, nc - 1 - ci if rev else ci, col))

    shape = jax.ShapeDtypeStruct((batch, seq, v_w), F32)
    state = pltpu.VMEM((RET_HEADS, RET_DK, RET_DV), F32)
    v_col = 2 * qk_w // v_w
    return pl.pallas_call(
        _ret_kernel,
        grid=(batch, nc),
        in_specs=[pl.BlockSpec(memory_space=pltpu.SMEM),
                  tok(qk_w, 0, False), tok(qk_w, 1, False), tok(v_w, v_col, False),
                  tok(qk_w, 0, True), tok(qk_w, 1, True), tok(v_w, v_col, True)],
        out_specs=[tok(v_w, 0, False), tok(v_w, 0, True)],
        out_shape=[shape, shape],
        scratch_shapes=[state, state],
        compiler_params=_params("parallel", "arbitrary"),
        name="retention",
    )(log_g, x, x, x, x, x, x)


def _ret_post_kernel(rf_ref, rb_ref, g_ref, gain_ref, y_ref):
    ret = rf_ref[...] + rb_ref[...]
    gate = g_ref[...]
    gate = gate * _sigmoid(gate)
    gain = gain_ref[...]
    for h in range(RET_HEADS):
        sl = slice(h * RET_DV, (h + 1) * RET_DV)
        y = ret[:, sl]
        mu = jnp.mean(y, axis=-1, keepdims=True)
        yc = y - mu
        var = jnp.mean(yc * yc, axis=-1, keepdims=True)
        y_ref[:, sl] = (gate[:, sl] * yc * lax.rsqrt(var + RET_NORM_EPS) * gain[:, sl]).astype(y_ref.dtype)


def _ret_post(ret_f, ret_b, rproj, gain):
    batch, seq, w = ret_f.shape
    m = batch * seq
    tm = ROW_TILE
    gcol = rproj.shape[1] // w - 1
    row = pl.BlockSpec((tm, w), lambda i: (i, 0))
    return pl.pallas_call(
        _ret_post_kernel,
        grid=(m // tm,),
        in_specs=[row, row, pl.BlockSpec((tm, w), lambda i: (i, gcol)), pl.BlockSpec((1, w), lambda i: (0, 0))],
        out_specs=row,
        out_shape=jax.ShapeDtypeStruct((m, w), BF16),
        compiler_params=_params("parallel"),
        name="ret_post",
    )(ret_f.reshape(m, w), ret_b.reshape(m, w), rproj, gain.reshape(1, w))


def _rwkv_prep_kernel(u_ref, up_ref, un_ref, wf_ref, conv_ref, w0_ref, w2_ref, a0_ref, a2_ref, g2_ref,
                      rkv_ref, lwa_ref, g_ref, *, ns):
    i = pl.program_id(1)
    ts, halo = u_ref.shape[0], up_ref.shape[0]
    xa = _dot(jnp.concatenate([up_ref[...], u_ref[...], un_ref[...]], axis=0), wf_ref[...])
    x = xa[halo:halo + ts]
    w = RWKV_WIDTH
    row = lax.broadcasted_iota(jnp.int32, (ts, 1), 0)
    prev_row = xa[halo - 1:halo, :] * jnp.where(i > 0, 1.0, 0.0)
    next_row = xa[halo + ts:halo + ts + 1, :] * jnp.where(i < ns - 1, 1.0, 0.0)
    x_m1 = jnp.where(row == 0, prev_row, pltpu.roll(x, 1, 0))
    x_p1 = jnp.where(row == ts - 1, next_row, pltpu.roll(x, ts - 1, 0))
    cw = conv_ref[...]
    feats = cw[0:1, :] * x_m1 + cw[1:2, :] * x + cw[2:3, :] * x_p1
    rkv_ref[...] = feats[:, :3 * w]
    w_lo = jnp.tanh(feats[:, 3 * w:3 * w + 128]).astype(BF16)
    a_lo = feats[:, 3 * w + 128:3 * w + 256].astype(BF16)
    g_lo = _sigmoid(feats[:, 3 * w + 256:3 * w + 384]).astype(BF16)
    logw = -_softplus(-(w0_ref[...] + _dot(w_lo, w2_ref[...]))) - 0.5
    lwa_ref[:, :2 * w] = -jnp.exp(logw)
    lwa_ref[:, 2 * w:] = _sigmoid(a0_ref[...] + _dot(a_lo, a2_ref[...]))
    g_ref[...] = _dot(g_lo, g2_ref[...])


def _rwkv_prep(u, w_feat, conv, w0, w2bd, a0, a2bd, g2, batch, seq):
    ts = 512
    halo = 16
    ns = seq // ts
    d, ch = w_feat.shape
    w = RWKV_WIDTH
    x = u.reshape(batch, seq, d)
    full = lambda shape: pl.BlockSpec(shape, lambda b, i: (0, 0))
    return pl.pallas_call(
        functools.partial(_rwkv_prep_kernel, ns=ns),
        grid=(batch, ns),
        in_specs=[
            pl.BlockSpec((None, ts, d), lambda b, i: (b, i, 0)),
            pl.BlockSpec((None, halo, d), lambda b, i: (b, jnp.maximum(i * (ts // halo) - 1, 0), 0)),
            pl.BlockSpec((None, halo, d), lambda b, i: (b, jnp.minimum((i + 1) * (ts // halo), seq // halo - 1), 0)),
            full((d, ch)),
            full((3, ch)), full((1, 2 * w)), full((128, 2 * w)), full((1, 2 * w)), full((128, 2 * w)), full((128, w)),
        ],
        out_specs=[
            pl.BlockSpec((None, ts, 3 * w), lambda b, i: (b, i, 0)),
            pl.BlockSpec((None, ts, 4 * w), lambda b, i: (b, i, 0)),
            pl.BlockSpec((None, ts, w), lambda b, i: (b, i, 0)),
        ],
        out_shape=[
            jax.ShapeDtypeStruct((batch, seq, 3 * w), F32),
            jax.ShapeDtypeStruct((batch, seq, 4 * w), F32),
            jax.ShapeDtypeStruct((batch, seq, w), F32),
        ],
        compiler_params=_params("parallel", "arbitrary"),
        name="rwkv_prep",
    )(x, x, x, w_feat, conv, w0, w2bd, a0, a2bd, g2)


def _rwkv_fold_decay(r, k, v, lw, a, kk_scale, k_a, ones_bd, reverse):
    c = RWKV_CHUNK
    sgn = -1 if reverse else 1
    d = (lax.broadcasted_iota(jnp.int32, (c, c), 0) - lax.broadcasted_iota(jnp.int32, (c, c), 1)) * sgn
    tri = jnp.where(d >= 0, 1.0, 0.0).astype(BF16)
    hi = lw.astype(BF16)
    rem = lw - hi.astype(F32)
    mid = rem.astype(BF16)
    lo = (rem - mid.astype(F32)).astype(BF16)
    cs = _dot(tri, hi) + _dot(tri, mid) + _dot(tri, lo)
    g_all = jnp.exp(jnp.sum(lw, axis=0, keepdims=True))
    e_neg = jnp.exp(-cs)
    kks = k * kk_scale
    sq_hi, sq_lo = _split2(kks * kks)
    kk = kks * lax.rsqrt(_dot(sq_hi, ones_bd) + _dot(sq_lo, ones_bd) + 1e-12)
    abar_all = -kk * jnp.exp(cs - lw)
    bbar_all = kk * a * e_neg
    kbar_all = k * (1.0 + (a - 1.0) * k_a) * e_neg
    rbar_all = r * jnp.exp(cs)
    return abar_all, rbar_all, bbar_all, kbar_all, v, g_all


def _rwkv_masks(reverse):
    n, gw = RWKV_N, RWKV_GROUP_HEADS * RWKV_N
    row = lax.broadcasted_iota(jnp.int32, (gw, gw), 0)
    col = lax.broadcasted_iota(jnp.int32, (gw, gw), 1)
    same_head = (row // n) == (col // n)
    dd = jnp.where(same_head, (row - col) * (-1 if reverse else 1), -1)
    return same_head, dd > 0, dd >= 0, jnp.where(row == col, 1.0, 0.0)


def _rwkv_group_chain(folded, masks, g, s_ref):
    c, n, gh = RWKV_CHUNK, RWKV_N, RWKV_GROUP_HEADS
    gw = gh * n
    abar_all, rbar_all, bbar_all, kbar_all, v, g_all = folded
    same_head, strict, incl, eye = masks
    tile = lambda x: jnp.concatenate([x] * gh, axis=0)
    n_sq = c.bit_length() - 1
    gl = slice(g * gw, (g + 1) * gw)
    a_s = jnp.where(same_head, tile(abar_all[:, gl]), 0.0)
    r_s = jnp.where(same_head, tile(rbar_all[:, gl]), 0.0)
    v_s = jnp.where(same_head, tile(v[:, gl]), 0.0).astype(BF16)
    b_t = tile(bbar_all[:, gl].astype(BF16))
    k_t = tile(kbar_all[:, gl].astype(BF16))
    bk = jnp.concatenate([b_t, k_t], axis=0)
    f = _dot_nt(jnp.concatenate([a_s, r_s], axis=0).astype(BF16), bk)
    yield
    l_b = jnp.where(strict, f[:gw, :gw], 0.0)
    l_k = jnp.where(strict, f[:gw, gw:], 0.0).astype(BF16)
    m_b = jnp.where(incl, f[gw:, :gw], 0.0).astype(BF16)
    m_k = jnp.where(incl, f[gw:, gw:], 0.0).astype(BF16)
    tt = eye + l_b
    pwb = l_b.astype(BF16)
    pwb = _dot(pwb, pwb).astype(BF16)
    yield
    lkv = _dot(l_k, v_s).astype(BF16)
    yield
    for _ in range(n_sq - 2):
        both = _dot(pwb, jnp.concatenate([pwb, tt.astype(BF16)], axis=1))
        yield
        tt = tt + both[:, gw:]
        pwb = both[:, :gw].astype(BF16)
    tt = tt + _dot(pwb, tt.astype(BF16))
    yield
    zzb = _dot(tt.astype(BF16), jnp.concatenate([a_s.astype(BF16), lkv], axis=1)).astype(BF16)
    yield
    w_b = zzb[:, :gw]
    uv = jnp.concatenate([zzb[:, gw:], v_s], axis=0)
    qt = r_s + _dot(m_b, w_b)
    yield
    yloc = _dot(jnp.concatenate([m_b, m_k], axis=1), uv)
    yield
    state = s_ref[g]
    s_b = state.astype(BF16)
    y_s = _dot_nt(qt.astype(BF16), s_b) + yloc
    yield
    y = y_s[0:c]
    for i in range(1, gh):
        y = y + y_s[i * c:(i + 1) * c]
    gdec = g_all[:, gl]
    zb = _dot_tn(zzb, b_t)
    yield
    gmat = jnp.where(same_head, zb[gw:] + _dot_tn(v_s, k_t), 0.0) * gdec
    yield
    p_off = jnp.where(same_head, zb[:gw], 0.0) * gdec
    s_ref[g] = state * gdec + _dot(s_b, p_off.astype(BF16)) + gmat
    return y


def _rwkv_chunk_kernel(rf_ref, kf_ref, vf_ref, lwf_ref, af_ref, rb_ref, kb_ref, vb_ref, lwb_ref, ab_ref,
                       kk_ref, ka_ref, ones_ref, of_ref, ob_ref, sf_ref, sb_ref):
    @pl.when(pl.program_id(1) == 0)
    def _():
        sf_ref[...] = jnp.zeros_like(sf_ref)
        sb_ref[...] = jnp.zeros_like(sb_ref)

    kk_scale, k_a, ones_bd = kk_ref[...], ka_ref[...], ones_ref[...]
    masks = (_rwkv_masks(False), _rwkv_masks(True))
    ngroups = RWKV_HEADS // RWKV_GROUP_HEADS
    chains = []
    for b in range(rf_ref.shape[0]):
        fwd = _rwkv_fold_decay(rf_ref[b], kf_ref[b], vf_ref[b], lwf_ref[b], af_ref[b], kk_scale, k_a, ones_bd, False)
        bwd = _rwkv_fold_decay(rb_ref[b], kb_ref[b], vb_ref[b], lwb_ref[b], ab_ref[b], kk_scale, k_a, ones_bd, True)
        for g in range(ngroups):
            chains.append((of_ref, b, g, _rwkv_group_chain(fwd, masks[0], g, sf_ref.at[b])))
            chains.append((ob_ref, b, g, _rwkv_group_chain(bwd, masks[1], g, sb_ref.at[b])))
    gw = RWKV_GROUP_HEADS * RWKV_N
    while chains:
        live = []
        for o_ref, b, g, chain in chains:
            try:
                next(chain)
                live.append((o_ref, b, g, chain))
            except StopIteration as done:
                o_ref[b, :, g * gw:(g + 1) * gw] = done.value
        chains = live


def _rwkv_scan(rkv, lwa, kk_scale, k_a, batch, seq):
    c = RWKV_CHUNK
    nc = seq // c
    w = RWKV_WIDTH
    gw = RWKV_GROUP_HEADS * RWKV_N
    nb = RWKV_BATCH_ROWS
    head = np.arange(w) // RWKV_N
    ones_bd = jnp.asarray(head[:, None] == head[None, :], BF16)

    def tok(col, rev):
        return pl.BlockSpec((nb, c, w), lambda b, ci: (b, nc - 1 - ci if rev else ci, col))

    vec = pl.BlockSpec((1, w), lambda b, ci: (0, 0))
    shape = jax.ShapeDtypeStruct((batch, seq, w), F32)
    state = pltpu.VMEM((nb, RWKV_HEADS // RWKV_GROUP_HEADS, gw, gw), F32)
    return pl.pallas_call(
        _rwkv_chunk_kernel,
        grid=(batch // nb, nc),
        in_specs=[
            tok(0, False), tok(1, False), tok(2, False), tok(0, False), tok(2, False),
            tok(0, True), tok(1, True), tok(2, True), tok(1, True), tok(3, True),
            vec, vec, pl.BlockSpec((w, w), lambda b, ci: (0, 0)),
        ],
        out_specs=[tok(0, False), tok(0, True)],
        out_shape=[shape, shape],
        scratch_shapes=[state, state],
        compiler_params=_params("parallel", "arbitrary"),
        name="rwkv_scan",
    )(rkv, rkv, rkv, lwa, lwa, rkv, rkv, rkv, lwa, lwa, kk_scale, k_a, ones_bd)


def _rwkv_post_kernel(yf_ref, yb_ref, r_ref, k_ref, v_ref, a0_ref, a1_ref, g_ref, ka_ref, rk_ref, lnw_ref, lnb_ref, o_ref):
    n = RWKV_N
    y = yf_ref[...] + yb_ref[...]
    k, v, ka = k_ref[...], v_ref[...], ka_ref[...]
    kd = k * (1.0 + (a0_ref[...] - 1.0) * ka) + k * (1.0 + (a1_ref[...] - 1.0) * ka)
    t = r_ref[...] * kd * rk_ref[...]
    lnw, lnb = lnw_ref[...], lnb_ref[...]
    outs = []
    for h in range(RWKV_HEADS):
        sl = slice(h * n, (h + 1) * n)
        yh = y[:, sl]
        mu = jnp.mean(yh, axis=-1, keepdims=True)
        yc = yh - mu
        var = jnp.mean(yc * yc, axis=-1, keepdims=True)
        bonus = jnp.sum(t[:, sl], axis=-1, keepdims=True)
        outs.append(yc * lax.rsqrt(var + RWKV_NORM_EPS) * lnw[:, sl] + lnb[:, sl] + bonus * v[:, sl])
    o_ref[...] = (jnp.concatenate(outs, axis=1) * g_ref[...]).astype(o_ref.dtype)


def _rwkv_post(yf, yb, rkv, lwa, g, k_a, r_k, ln_w, ln_b):
    batch, seq, w = yf.shape
    m = batch * seq
    tm = ROW_TILE
    rkv2 = rkv.reshape(m, 3 * w)
    lwa2 = lwa.reshape(m, 4 * w)
    tok = lambda col: pl.BlockSpec((tm, w), lambda i: (i, col))
    vec = pl.BlockSpec((1, w), lambda i: (0, 0))
    return pl.pallas_call(
        _rwkv_post_kernel,
        grid=(m // tm,),
        in_specs=[tok(0), tok(0), tok(0), tok(1), tok(2), tok(2), tok(3), tok(0), vec, vec, vec, vec],
        out_specs=pl.BlockSpec((tm, w), lambda i: (i, 0)),
        out_shape=jax.ShapeDtypeStruct((m, w), BF16),
        compiler_params=_params("parallel"),
        name="rwkv_post",
    )(yf.reshape(m, w), yb.reshape(m, w), rkv2, rkv2, rkv2, lwa2, lwa2, g.reshape(m, w), k_a, r_k, ln_w, ln_b)


def _mix_out_kernel(x_ref, u_ref, ya_ref, yb_ref, yc_ref, ga_ref, gb_ref, gc_ref, wa_ref, wb_ref, wc_ref, wo_ref,
                    o_ref, m_ref, *, nj):
    j = pl.program_id(1)
    tn = o_ref.shape[1]

    @pl.when(j < nj)
    def _():
        u = u_ref[...]
        acc = _sigmoid(_dot(u, ga_ref[...])) * _dot(ya_ref[...], wa_ref[...])
        acc += _sigmoid(_dot(u, gb_ref[...])) * _dot(yb_ref[...], wb_ref[...])
        acc += _sigmoid(_dot(u, gc_ref[...])) * _dot(yc_ref[...], wc_ref[...])
        m_ref[j] = acc.astype(BF16)

    @pl.when(j >= nj)
    def _():
        acc = x_ref[...]
        for k in range(nj):
            acc += _dot(m_ref[k], wo_ref[k * tn:(k + 1) * tn, :])
        o_ref[...] = acc


def _mix_out(x, u, ya, yb, yc, w_gates, wa, wb, wc, w_out):
    m, d = x.shape
    tm, tn = ROW_TILE, 512
    nj = d // tn
    first = lambda j: jnp.minimum(j, nj - 1)
    second = lambda j: jnp.maximum(j - nj, 0)
    act = lambda a: pl.BlockSpec((tm, a.shape[1]), lambda i, j: (i, 0))
    wsp = lambda a: pl.BlockSpec((a.shape[0], tn), lambda i, j: (0, first(j)))
    gsp = lambda b: pl.BlockSpec((d, tn), lambda i, j: (0, b * nj + first(j)))
    return pl.pallas_call(
        functools.partial(_mix_out_kernel, nj=nj),
        grid=(m // tm, 2 * nj),
        in_specs=[pl.BlockSpec((tm, tn), lambda i, j: (i, second(j))),
                  act(u), act(ya), act(yb), act(yc), gsp(0), gsp(1), gsp(2), wsp(wa), wsp(wb), wsp(wc),
                  pl.BlockSpec((d, tn), lambda i, j: (0, second(j)))],
        out_specs=pl.BlockSpec((tm, tn), lambda i, j: (i, second(j))),
        out_shape=jax.ShapeDtypeStruct((m, d), F32),
        scratch_shapes=[pltpu.VMEM((nj, tm, tn), BF16)],
        compiler_params=_params("parallel", "arbitrary"),
        name="mix_out",
    )(x, u, ya, yb, yc, w_gates, w_gates, w_gates, wa, wb, wc, w_out)


def _block_diag2(w2):
    zero = jnp.zeros_like(w2[0])
    return jnp.concatenate([jnp.concatenate([w2[0], zero], axis=1), jnp.concatenate([zero, w2[1]], axis=1)], axis=0)


def _mixers(x, u, p, batch, seq):
    outs, lses = [], []
    for g in range(len(ATT_GROUPS)):
        qkv = _matmul(u, p["w_qkv"][g], 768, BF16)
        o, l = _attention_group(qkv, g, batch, seq)
        outs.append(o)
        lses.append(l)
    ya = _att_merge(outs, lses)

    rproj = _matmul(u, p["w_ret"], 1024, F32)
    ret_f, ret_b = _retention(rproj, p["ret_log_g"], batch, seq)
    yb = _ret_post(ret_f, ret_b, rproj, p["ret_norm"])

    rkv, lwa, g = _rwkv_prep(u, p["w_rwkv"], p["rwkv_conv"], p["rwkv_w0"], p["rwkv_w2bd"], p["rwkv_a0"], p["rwkv_a2bd"],
                             p["rwkv_g2"], batch, seq)
    yf, yb_dir = _rwkv_scan(rkv, lwa, p["rwkv_k_k"], p["rwkv_k_a"], batch, seq)
    yc = _rwkv_post(yf, yb_dir, rkv, lwa, g, p["rwkv_k_a"], p["rwkv_r_k"], p["rwkv_ln_w"], p["rwkv_ln_b"])

    return _mix_out(x, u, ya, yb, yc, p["w_gates"], p["w_branch_a"], p["w_branch_b"], p["w_branch_c"], p["w_out"])


def kernel(x_prompt, x_sample, ffn1_norm, ffn1_w_gate, ffn1_w_up, ffn1_w_down, mix_norm, w_in, ret_decay_logit, ret_norm, rwkv_conv, rwkv_w0, rwkv_w2, rwkv_a0, rwkv_a2, rwkv_g2, rwkv_k_k, rwkv_k_a, rwkv_r_k, rwkv_ln_w, rwkv_ln_b, w_branch_a, w_branch_b, w_branch_c, w_out, ffn2_norm, ffn2_w_gate, ffn2_w_up, ffn2_w_down, final_norm):
    depth = ffn1_norm.shape[0]
    w = RWKV_WIDTH
    ng, aw = len(ATT_GROUPS), ATT_OUT_WIDTH
    att3 = 3 * ng * aw
    ret_w = 2 * RET_HEADS * RET_DK + 2 * RET_HEADS * RET_DV
    layers = []
    for l in range(depth):
        w_in_l = w_in[l].astype(BF16)
        o1, o2, o3 = att3, att3 + ret_w, att3 + ret_w + RWKV_CONV_CH
        w_qkv = [jnp.concatenate([w_in_l[:, (t * ng + g) * aw:(t * ng + g + 1) * aw] for t in range(3)], axis=1)
                 for g in range(ng)]
        layers.append({
            "ffn1": (ffn1_norm[l], ffn1_w_gate[l].astype(BF16), ffn1_w_up[l].astype(BF16), ffn1_w_down[l].astype(BF16)),
            "ffn2": (ffn2_norm[l], ffn2_w_gate[l].astype(BF16), ffn2_w_up[l].astype(BF16), ffn2_w_down[l].astype(BF16)),
            "mix_norm": mix_norm[l],
            "w_qkv": w_qkv, "w_ret": w_in_l[:, o1:o2], "w_rwkv": w_in_l[:, o2:o3], "w_gates": w_in_l[:, o3:],
            "ret_log_g": jax.nn.log_sigmoid(ret_decay_logit[l].astype(F32)),
            "ret_norm": ret_norm[l].astype(F32),
            "rwkv_conv": rwkv_conv[l].astype(F32),
            "rwkv_w0": rwkv_w0[l].astype(F32).reshape(1, 2 * w),
            "rwkv_w2bd": _block_diag2(rwkv_w2[l]).astype(BF16),
            "rwkv_a0": rwkv_a0[l].astype(F32).reshape(1, 2 * w),
            "rwkv_a2bd": _block_diag2(rwkv_a2[l]).astype(BF16),
            "rwkv_g2": rwkv_g2[l].astype(BF16),
            "rwkv_k_k": rwkv_k_k[l].astype(F32).reshape(1, w),
            "rwkv_k_a": rwkv_k_a[l].astype(F32).reshape(1, w),
            "rwkv_r_k": rwkv_r_k[l].astype(F32).reshape(1, w),
            "rwkv_ln_w": rwkv_ln_w[l].astype(F32).reshape(1, w),
            "rwkv_ln_b": rwkv_ln_b[l].astype(F32).reshape(1, w),
            "w_branch_a": w_branch_a[l].astype(BF16), "w_branch_b": w_branch_b[l].astype(BF16),
            "w_branch_c": w_branch_c[l].astype(BF16), "w_out": w_out[l].astype(BF16),
        })

    def run(x):
        batch, seq, d = x.shape
        y = x.reshape(batch * seq, d)
        for l, p in enumerate(layers):
            y, u = _ffn(y, *p["ffn1"], p["mix_norm"], "next_norm")
            y = _mixers(y, u, p, batch, seq)
            y, = _ffn(y, *p["ffn2"], final_norm, "final" if l == depth - 1 else "plain")
        return y.reshape(batch, seq, d)

    return (run(x_prompt), run(x_sample))
```

```python
import functools

import numpy as np
import jax
import jax.numpy as jnp
from jax import lax
from jax.experimental import pallas as pl
from jax.experimental.pallas import tpu as pltpu

F32 = jnp.float32
BF16 = jnp.bfloat16

NORM_EPS = 1e-6

ATT_GROUPS = ((128, 1), (512, 4), (2048, 16))
ATT_HEADS = 4
ATT_HEAD_DIM = 128
ATT_OUT_WIDTH = ATT_HEADS * ATT_HEAD_DIM
ATT_HALF = 64
ATT_TQ = 128
ATT_BLOCK = 512

RET_HEADS = 4
RET_DK = 128
RET_DV = 256
RET_CHUNK = 128
RET_NORM_EPS = 1e-5

RWKV_HEADS = 8
RWKV_N = 64
RWKV_WIDTH = RWKV_HEADS * RWKV_N
RWKV_CONV_CH = 1920
RWKV_NORM_EPS = 64e-5
RWKV_CHUNK = 64
RWKV_GROUP_HEADS = 4
RWKV_BATCH_ROWS = 2

ROW_TILE = 512
MM_ROW_TILE = 1024
VMEM_LIMIT_BYTES = 56 * 1024 * 1024
NEG_BIG = -1e30


def _params(*sem):
    return pltpu.CompilerParams(dimension_semantics=sem, vmem_limit_bytes=VMEM_LIMIT_BYTES)


def _dot(a, b):
    return jnp.dot(a, b, preferred_element_type=F32)


def _dot_nt(a, b):
    return lax.dot_general(a, b, (((1,), (1,)), ((), ())), preferred_element_type=F32)


def _dot_tn(a, b):
    return lax.dot_general(a, b, (((0,), (0,)), ((), ())), preferred_element_type=F32)


def _split2(a):
    hi = a.astype(BF16)
    lo = (a - hi.astype(F32)).astype(BF16)
    return hi, lo


def _rms(x, gain):
    ms = jnp.mean(x * x, axis=-1, keepdims=True)
    return x * lax.rsqrt(ms + NORM_EPS) * gain


def _sigmoid(x):
    return 1.0 / (1.0 + jnp.exp(-x))


def _softplus(x):
    return jnp.maximum(x, 0.0) + jnp.log(1.0 + jnp.exp(-jnp.abs(x)))


def _ffn_kernel(x_ref, g_ref, wg_ref, wu_ref, wd_ref, g2_ref, o_ref, *rest, nf, mode):
    xn_ref = rest[-1]
    f = pl.program_id(1)

    @pl.when(f == 0)
    def _():
        xn_ref[...] = _rms(x_ref[...], g_ref[...]).astype(BF16)
        o_ref[...] = jnp.zeros_like(o_ref)

    xn = xn_ref[...]
    hg = _dot(xn, wg_ref[...])
    hu = _dot(xn, wu_ref[...])
    h = (hg * _sigmoid(hg) * hu).astype(BF16)
    o_ref[...] += _dot(h, wd_ref[...])

    @pl.when(f == nf - 1)
    def _():
        y = x_ref[...] + 0.5 * o_ref[...]
        if mode == "final":
            y = _rms(y, g2_ref[...])
        o_ref[...] = y
        if mode == "next_norm":
            rest[0][...] = _rms(y, g2_ref[...]).astype(BF16)


def _ffn(x, gain, wg, wu, wd, gain2, mode):
    m, d = x.shape
    ff = wg.shape[1]
    tm, tf = (ROW_TILE, 512) if mode == "next_norm" else (MM_ROW_TILE, 256)
    nf = ff // tf
    row = pl.BlockSpec((tm, d), lambda i, f: (i, 0))
    vec = pl.BlockSpec((1, d), lambda i, f: (0, 0))
    out_specs, out_shape = [row], [jax.ShapeDtypeStruct((m, d), F32)]
    if mode == "next_norm":
        out_specs.append(row)
        out_shape.append(jax.ShapeDtypeStruct((m, d), BF16))
    return pl.pallas_call(
        functools.partial(_ffn_kernel, nf=nf, mode=mode),
        grid=(m // tm, nf),
        in_specs=[row, vec,
                  pl.BlockSpec((d, tf), lambda i, f: (0, f)),
                  pl.BlockSpec((d, tf), lambda i, f: (0, f)),
                  pl.BlockSpec((tf, d), lambda i, f: (f, 0)),
                  vec],
        out_specs=out_specs,
        out_shape=out_shape,
        scratch_shapes=[pltpu.VMEM((tm, d), BF16)],
        compiler_params=_params("parallel", "arbitrary"),
        name="ffn",
    )(x, gain.reshape(1, d), wg, wu, wd, gain2.reshape(1, d))


def _matmul_kernel(x_ref, w_ref, o_ref):
    o_ref[...] = _dot(x_ref[...], w_ref[...]).astype(o_ref.dtype)


def _matmul(x, w, tn, out_dtype):
    m, d = x.shape
    n = w.shape[1]
    tm = MM_ROW_TILE
    return pl.pallas_call(
        _matmul_kernel,
        grid=(m // tm, n // tn),
        in_specs=[pl.BlockSpec((tm, d), lambda i, j: (i, 0)), pl.BlockSpec((d, tn), lambda i, j: (0, j))],
        out_specs=pl.BlockSpec((tm, tn), lambda i, j: (i, j)),
        out_shape=jax.ShapeDtypeStruct((m, n), out_dtype),
        compiler_params=_params("parallel", "arbitrary"),
        name="proj",
    )(x, w)


def _attn_kernel(q_ref, kp_ref, kc_ref, kn_ref, vp_ref, vc_ref, vn_ref, o_ref, l_ref, *, dil, slopes, sub_len):
    tq, half, dh = ATT_TQ, ATT_HALF, ATT_HEAD_DIM
    tqb = q_ref.shape[0]
    i, phase = pl.program_id(1), pl.program_id(2)
    out_rows = lambda j: pl.ds(j * tq * dil + phase, tq, stride=dil) if dil > 1 else pl.ds(j * tq, tq)
    qi = lax.broadcasted_iota(jnp.int32, (tq, tq + 2 * half), 0)
    kj = lax.broadcasted_iota(jnp.int32, (tq, tq + 2 * half), 1)
    rel = kj - half - qi
    band = jnp.where(jnp.abs(rel) <= half, 1, 0)
    dist = (dil * jnp.abs(rel)).astype(F32)
    scale = dh ** -0.5
    oks = []
    for j in range(tqb // tq):
        kidx = i * tqb + j * tq - half + kj
        oks.append(band * jnp.where(kidx >= 0, 1, 0) * jnp.where(kidx < sub_len, 1, 0) > 0)
    for h in range(ATT_HEADS):
        sl = slice(h * dh, (h + 1) * dh)
        kfull = jnp.concatenate([kp_ref[:, sl], kc_ref[:, sl], kn_ref[:, sl]], axis=0)
        vfull = jnp.concatenate([vp_ref[:, sl], vc_ref[:, sl], vn_ref[:, sl]], axis=0)
        nsub = tqb // tq
        rows = [slice(j * tq, (j + 1) * tq) for j in range(nsub)]
        keys = [slice(j * tq, (j + 1) * tq + 2 * half) for j in range(nsub)]
        scores = [_dot_nt(q_ref[rows[j], sl], kfull[keys[j]]) for j in range(nsub)]
        probs, dens = [], []
        for j in range(nsub):
            s = jnp.where(oks[j], scores[j] * scale - slopes[h] * dist, NEG_BIG)
            mx = jnp.max(s, axis=-1, keepdims=True)
            p = jnp.exp(s - mx)
            den = jnp.sum(p, axis=-1, keepdims=True)
            l_ref[h, out_rows(j), :] = jnp.broadcast_to(mx + jnp.log(den), (tq, dh))
            probs.append(p.astype(BF16))
            dens.append(den)
        for j in range(nsub):
            o_ref[h, out_rows(j), :] = _dot(probs[j], vfull[keys[j]]) / dens[j]


def _attention_group(qkv, g, batch, seq):
    dil = ATT_GROUPS[g][1]
    sub_len = seq // dil
    tqb = min(ATT_BLOCK, sub_len)
    nq = sub_len // tqb
    half = ATT_HALF
    w = ATT_OUT_WIDTH
    n = len(ATT_GROUPS) * ATT_HEADS
    slopes_all = np.power(np.float32(2.0), -8.0 * np.arange(1, n + 1, dtype=np.float32) / n)
    slopes = tuple(float(s) for s in slopes_all[g * ATT_HEADS:(g + 1) * ATT_HEADS])
    x = qkv.reshape(batch, sub_len, dil * 3 * w)
    per, last = tqb // half, sub_len // half - 1

    def spec(col, where):
        if where == 0:
            return pl.BlockSpec((None, tqb, w), lambda b, i, p: (b, i, p * 3 + col))
        if where < 0:
            return pl.BlockSpec((None, half, w), lambda b, i, p: (b, jnp.maximum(i * per - 1, 0), p * 3 + col))
        return pl.BlockSpec((None, half, w), lambda b, i, p: (b, jnp.minimum((i + 1) * per, last), p * 3 + col))

    nh, dh = ATT_HEADS, ATT_HEAD_DIM
    out_spec = pl.BlockSpec((nh, None, tqb * dil, dh), lambda b, i, p: (0, b, i, 0))
    out_shape = jax.ShapeDtypeStruct((nh, batch, seq, dh), F32)
    o, lse = pl.pallas_call(
        functools.partial(_attn_kernel, dil=dil, slopes=slopes, sub_len=sub_len),
        grid=(batch, nq, dil),
        in_specs=[spec(0, 0), spec(1, -1), spec(1, 0), spec(1, 1), spec(2, -1), spec(2, 0), spec(2, 1)],
        out_specs=[out_spec, out_spec],
        out_shape=[out_shape, out_shape],
        compiler_params=_params("parallel", "parallel", "arbitrary"),
        name=f"dilated_attn_g{g}",
    )(x, x, x, x, x, x, x)
    return o.reshape(nh, batch * seq, dh), lse.reshape(nh, batch * seq, dh)


def _att_merge_kernel(o0, o1, o2, l0, l1, l2, y_ref):
    dh = ATT_HEAD_DIM
    for h in range(ATT_HEADS):
        a, b, c = l0[h], l1[h], l2[h]
        mx = jnp.maximum(jnp.maximum(a, b), c)
        ea, eb, ec = jnp.exp(a - mx), jnp.exp(b - mx), jnp.exp(c - mx)
        y = (ea * o0[h] + eb * o1[h] + ec * o2[h]) / (ea + eb + ec)
        y_ref[:, h * dh:(h + 1) * dh] = y.astype(y_ref.dtype)


def _att_merge(outs, lses):
    nh, m, dh = outs[0].shape
    tm = ROW_TILE
    spec = pl.BlockSpec((nh, tm, dh), lambda i: (0, i, 0))
    return pl.pallas_call(
        _att_merge_kernel,
        grid=(m // tm,),
        in_specs=[spec] * 6,
        out_specs=pl.BlockSpec((tm, nh * dh), lambda i: (i, 0)),
        out_shape=jax.ShapeDtypeStruct((m, nh * dh), BF16),
        compiler_params=_params("parallel"),
        name="att_merge",
    )(*outs, *lses)


def _ret_kernel(lg_ref, qf_ref, kf_ref, vf_ref, qb_ref, kb_ref, vb_ref, of_ref, ob_ref, sf_ref, sb_ref):
    c, dk, dv = RET_CHUNK, RET_DK, RET_DV

    @pl.when(pl.program_id(1) == 0)
    def _():
        sf_ref[...] = jnp.zeros_like(sf_ref)
        sb_ref[...] = jnp.zeros_like(sb_ref)

    pos = lax.broadcasted_iota(jnp.int32, (c, 1), 0).astype(F32)
    diff = (lax.broadcasted_iota(jnp.int32, (c, c), 0) - lax.broadcasted_iota(jnp.int32, (c, c), 1)).astype(F32)
    up = jnp.maximum(diff, 0.0)
    dn = jnp.maximum(-diff, 0.0)
    for h in range(RET_HEADS):
        ks, vs = slice(h * dk, (h + 1) * dk), slice(h * dv, (h + 1) * dv)
        lgf = jnp.full((1, 1), lg_ref[0, h], F32)
        lgb = jnp.full((1, 1), lg_ref[1, h], F32)

        q = qf_ref[:, ks].astype(BF16)
        k = kf_ref[:, ks] * (dk ** -0.5)
        v = vf_ref[:, vs].astype(BF16)
        state = sf_ref[h]
        dmat = jnp.where(diff >= 0, jnp.exp(up * lgf), 0.0) + jnp.where(diff <= 0, jnp.exp(dn * lgb), 0.0)
        scores = _dot_nt(q, k.astype(BF16)) * dmat
        of_ref[:, vs] = _dot(scores.astype(BF16), v) + _dot(q, state.astype(BF16)) * jnp.exp((pos + 1.0) * lgf)
        kz = (k * jnp.exp((c - 1.0 - pos) * lgf)).astype(BF16)
        sf_ref[h] = jnp.exp(c * lgf) * state + _dot_tn(kz, v)

        q = qb_ref[:, ks].astype(BF16)
        k = kb_ref[:, ks] * (dk ** -0.5)
        v = vb_ref[:, vs].astype(BF16)
        state = sb_ref[h]
        ob_ref[:, vs] = _dot(q, state.astype(BF16)) * jnp.exp((c - pos) * lgb)
        kz = (k * jnp.exp(pos * lgb)).astype(BF16)
        sb_ref[h] = jnp.exp(c * lgb) * state + _dot_tn(kz, v)


def _retention(rproj, log_g, batch, seq):
    c = RET_CHUNK
    nc = seq // c
    qk_w, v_w = RET_HEADS * RET_DK, RET_HEADS * RET_DV
    x = rproj.reshape(batch, seq, rproj.shape[1])

    def tok(width, col, rev):
        return pl.BlockSpec((None, c, width), lambda b, ci: (b, nc - 1 - ci if rev else ci, col))

    shape = jax.ShapeDtypeStruct((batch, seq, v_w), F32)
    state = pltpu.VMEM((RET_HEADS, RET_DK, RET_DV), F32)
    v_col = 2 * qk_w // v_w
    return pl.pallas_call(
        _ret_kernel,
        grid=(batch, nc),
        in_specs=[pl.BlockSpec(memory_space=pltpu.SMEM),
                  tok(qk_w, 0, False), tok(qk_w, 1, False), tok(v_w, v_col, False),
                  tok(qk_w, 0, True), tok(qk_w, 1, True), tok(v_w, v_col, True)],
        out_specs=[tok(v_w, 0, False), tok(v_w, 0, True)],
        out_shape=[shape, shape],
        scratch_shapes=[state, state],
        compiler_params=_params("parallel", "arbitrary"),
        name="retention",
    )(log_g, x, x, x, x, x, x)


def _ret_post_kernel(rf_ref, rb_ref, g_ref, gain_ref, y_ref):
    ret = rf_ref[...] + rb_ref[...]
    gate = g_ref[...]
    gate = gate * _sigmoid(gate)
    gain = gain_ref[...]
    for h in range(RET_HEADS):
        sl = slice(h * RET_DV, (h + 1) * RET_DV)
        y = ret[:, sl]
        mu = jnp.mean(y, axis=-1, keepdims=True)
        yc = y - mu
        var = jnp.mean(yc * yc, axis=-1, keepdims=True)
        y_ref[:, sl] = (gate[:, sl] * yc * lax.rsqrt(var + RET_NORM_EPS) * gain[:, sl]).astype(y_ref.dtype)


def _ret_post(ret_f, ret_b, rproj, gain):
    batch, seq, w = ret_f.shape
    m = batch * seq
    tm = ROW_TILE
    gcol = rproj.shape[1] // w - 1
    row = pl.BlockSpec((tm, w), lambda i: (i, 0))
    return pl.pallas_call(
        _ret_post_kernel,
        grid=(m // tm,),
        in_specs=[row, row, pl.BlockSpec((tm, w), lambda i: (i, gcol)), pl.BlockSpec((1, w), lambda i: (0, 0))],
        out_specs=row,
        out_shape=jax.ShapeDtypeStruct((m, w), BF16),
        compiler_params=_params("parallel"),
        name="ret_post",
    )(ret_f.reshape(m, w), ret_b.reshape(m, w), rproj, gain.reshape(1, w))


def _rwkv_prep_kernel(u_ref, up_ref, un_ref, wf_ref, conv_ref, w0_ref, w2_ref, a0_ref, a2_ref, g2_ref,
                      rkv_ref, lwa_ref, g_ref, *, ns):
    i = pl.program_id(1)
    ts, halo = u_ref.shape[0], up_ref.shape[0]
    xa = _dot(jnp.concatenate([up_ref[...], u_ref[...], un_ref[...]], axis=0), wf_ref[...])
    x = xa[halo:halo + ts]
    w = RWKV_WIDTH
    row = lax.broadcasted_iota(jnp.int32, (ts, 1), 0)
    prev_row = xa[halo - 1:halo, :] * jnp.where(i > 0, 1.0, 0.0)
    next_row = xa[halo + ts:halo + ts + 1, :] * jnp.where(i < ns - 1, 1.0, 0.0)
    x_m1 = jnp.where(row == 0, prev_row, pltpu.roll(x, 1, 0))
    x_p1 = jnp.where(row == ts - 1, next_row, pltpu.roll(x, ts - 1, 0))
    cw = conv_ref[...]
    feats = cw[0:1, :] * x_m1 + cw[1:2, :] * x + cw[2:3, :] * x_p1
    rkv_ref[...] = feats[:, :3 * w]
    w_lo = jnp.tanh(feats[:, 3 * w:3 * w + 128]).astype(BF16)
    a_lo = feats[:, 3 * w + 128:3 * w + 256].astype(BF16)
    g_lo = _sigmoid(feats[:, 3 * w + 256:3 * w + 384]).astype(BF16)
    logw = -_softplus(-(w0_ref[...] + _dot(w_lo, w2_ref[...]))) - 0.5
    lwa_ref[:, :2 * w] = -jnp.exp(logw)
    lwa_ref[:, 2 * w:] = _sigmoid(a0_ref[...] + _dot(a_lo, a2_ref[...]))
    g_ref[...] = _dot(g_lo, g2_ref[...])


def _rwkv_prep(u, w_feat, conv, w0, w2bd, a0, a2bd, g2, batch, seq):
    ts = 512
    halo = 16
    ns = seq // ts
    d, ch = w_feat.shape
    w = RWKV_WIDTH
    x = u.reshape(batch, seq, d)
    full = lambda shape: pl.BlockSpec(shape, lambda b, i: (0, 0))
    return pl.pallas_call(
        functools.partial(_rwkv_prep_kernel, ns=ns),
        grid=(batch, ns),
        in_specs=[
            pl.BlockSpec((None, ts, d), lambda b, i: (b, i, 0)),
            pl.BlockSpec((None, halo, d), lambda b, i: (b, jnp.maximum(i * (ts // halo) - 1, 0), 0)),
            pl.BlockSpec((None, halo, d), lambda b, i: (b, jnp.minimum((i + 1) * (ts // halo), seq // halo - 1), 0)),
            full((d, ch)),
            full((3, ch)), full((1, 2 * w)), full((128, 2 * w)), full((1, 2 * w)), full((128, 2 * w)), full((128, w)),
        ],
        out_specs=[
            pl.BlockSpec((None, ts, 3 * w), lambda b, i: (b, i, 0)),
            pl.BlockSpec((None, ts, 4 * w), lambda b, i: (b, i, 0)),
            pl.BlockSpec((None, ts, w), lambda b, i: (b, i, 0)),
        ],
        out_shape=[
            jax.ShapeDtypeStruct((batch, seq, 3 * w), F32),
            jax.ShapeDtypeStruct((batch, seq, 4 * w), F32),
            jax.ShapeDtypeStruct((batch, seq, w), F32),
        ],
        compiler_params=_params("parallel", "arbitrary"),
        name="rwkv_prep",
    )(x, x, x, w_feat, conv, w0, w2bd, a0, a2bd, g2)


def _rwkv_fold_decay(r, k, v, lw, a, kk_scale, k_a, ones_bd, reverse):
    c = RWKV_CHUNK
    sgn = -1 if reverse else 1
    d = (lax.broadcasted_iota(jnp.int32, (c, c), 0) - lax.broadcasted_iota(jnp.int32, (c, c), 1)) * sgn
    tri = jnp.where(d >= 0, 1.0, 0.0).astype(BF16)
    hi = lw.astype(BF16)
    rem = lw - hi.astype(F32)
    mid = rem.astype(BF16)
    lo = (rem - mid.astype(F32)).astype(BF16)
    cs = _dot(tri, hi) + _dot(tri, mid) + _dot(tri, lo)
    g_all = jnp.exp(jnp.sum(lw, axis=0, keepdims=True))
    e_neg = jnp.exp(-cs)
    kks = k * kk_scale
    sq_hi, sq_lo = _split2(kks * kks)
    kk = kks * lax.rsqrt(_dot(sq_hi, ones_bd) + _dot(sq_lo, ones_bd) + 1e-12)
    abar_all = -kk * jnp.exp(cs - lw)
    bbar_all = kk * a * e_neg
    kbar_all = k * (1.0 + (a - 1.0) * k_a) * e_neg
    rbar_all = r * jnp.exp(cs)
    return abar_all, rbar_all, bbar_all, kbar_all, v, g_all


def _rwkv_masks(reverse):
    n, gw = RWKV_N, RWKV_GROUP_HEADS * RWKV_N
    row = lax.broadcasted_iota(jnp.int32, (gw, gw), 0)
    col = lax.broadcasted_iota(jnp.int32, (gw, gw), 1)
    same_head = (row // n) == (col // n)
    dd = jnp.where(same_head, (row - col) * (-1 if reverse else 1), -1)
    return same_head, dd > 0, dd >= 0, jnp.where(row == col, 1.0, 0.0)


def _rwkv_group_chain(folded, masks, g, s_ref):
    c, n, gh = RWKV_CHUNK, RWKV_N, RWKV_GROUP_HEADS
    gw = gh * n
    abar_all, rbar_all, bbar_all, kbar_all, v, g_all = folded
    same_head, strict, incl, eye = masks
    tile = lambda x: jnp.concatenate([x] * gh, axis=0)
    n_sq = c.bit_length() - 1
    gl = slice(g * gw, (g + 1) * gw)
    a_s = jnp.where(same_head, tile(abar_all[:, gl]), 0.0)
    r_s = jnp.where(same_head, tile(rbar_all[:, gl]), 0.0)
    v_s = jnp.where(same_head, tile(v[:, gl]), 0.0).astype(BF16)
    b_t = tile(bbar_all[:, gl].astype(BF16))
    k_t = tile(kbar_all[:, gl].astype(BF16))
    bk = jnp.concatenate([b_t, k_t], axis=0)
    f = _dot_nt(jnp.concatenate([a_s, r_s], axis=0).astype(BF16), bk)
    yield
    l_b = jnp.where(strict, f[:gw, :gw], 0.0)
    l_k = jnp.where(strict, f[:gw, gw:], 0.0).astype(BF16)
    m_b = jnp.where(incl, f[gw:, :gw], 0.0).astype(BF16)
    m_k = jnp.where(incl, f[gw:, gw:], 0.0).astype(BF16)
    tt = eye + l_b
    pwb = l_b.astype(BF16)
    pwb = _dot(pwb, pwb).astype(BF16)
    yield
    lkv = _dot(l_k, v_s).astype(BF16)
    yield
    for _ in range(n_sq - 2):
        both = _dot(pwb, jnp.concatenate([pwb, tt.astype(BF16)], axis=1))
        yield
        tt = tt + both[:, gw:]
        pwb = both[:, :gw].astype(BF16)
    tt = tt + _dot(pwb, tt.astype(BF16))
    yield
    zzb = _dot(tt.astype(BF16), jnp.concatenate([a_s.astype(BF16), lkv], axis=1)).astype(BF16)
    yield
    w_b = zzb[:, :gw]
    uv = jnp.concatenate([zzb[:, gw:], v_s], axis=0)
    qt = r_s + _dot(m_b, w_b)
    yield
    yloc = _dot(jnp.concatenate([m_b, m_k], axis=1), uv)
    yield
    state = s_ref[g]
    s_b = state.astype(BF16)
    y_s = _dot_nt(qt.astype(BF16), s_b) + yloc
    yield
    y = y_s[0:c]
    for i in range(1, gh):
        y = y + y_s[i * c:(i + 1) * c]
    gdec = g_all[:, gl]
    zb = _dot_tn(zzb, b_t)
    yield
    gmat = jnp.where(same_head, zb[gw:] + _dot_tn(v_s, k_t), 0.0) * gdec
    yield
    p_off = jnp.where(same_head, zb[:gw], 0.0) * gdec
    s_ref[g] = state * gdec + _dot(s_b, p_off.astype(BF16)) + gmat
    return y


def _rwkv_chunk_kernel(rf_ref, kf_ref, vf_ref, lwf_ref, af_ref, rb_ref, kb_ref, vb_ref, lwb_ref, ab_ref,
                       kk_ref, ka_ref, ones_ref, of_ref, ob_ref, sf_ref, sb_ref):
    @pl.when(pl.program_id(1) == 0)
    def _():
        sf_ref[...] = jnp.zeros_like(sf_ref)
        sb_ref[...] = jnp.zeros_like(sb_ref)

    kk_scale, k_a, ones_bd = kk_ref[...], ka_ref[...], ones_ref[...]
    masks = (_rwkv_masks(False), _rwkv_masks(True))
    ngroups = RWKV_HEADS // RWKV_GROUP_HEADS
    chains = []
    for b in range(rf_ref.shape[0]):
        fwd = _rwkv_fold_decay(rf_ref[b], kf_ref[b], vf_ref[b], lwf_ref[b], af_ref[b], kk_scale, k_a, ones_bd, False)
        bwd = _rwkv_fold_decay(rb_ref[b], kb_ref[b], vb_ref[b], lwb_ref[b], ab_ref[b], kk_scale, k_a, ones_bd, True)
        for g in range(ngroups):
            chains.append((of_ref, b, g, _rwkv_group_chain(fwd, masks[0], g, sf_ref.at[b])))
            chains.append((ob_ref, b, g, _rwkv_group_chain(bwd, masks[1], g, sb_ref.at[b])))
    gw = RWKV_GROUP_HEADS * RWKV_N
    while chains:
        live = []
        for o_ref, b, g, chain in chains:
            try:
                next(chain)
                live.append((o_ref, b, g, chain))
            except StopIteration as done:
                o_ref[b, :, g * gw:(g + 1) * gw] = done.value
        chains = live


def _rwkv_scan(rkv, lwa, kk_scale, k_a, batch, seq):
    c = RWKV_CHUNK
    nc = seq // c
    w = RWKV_WIDTH
    gw = RWKV_GROUP_HEADS * RWKV_N
    nb = RWKV_BATCH_ROWS
    head = np.arange(w) // RWKV_N
    ones_bd = jnp.asarray(head[:, None] == head[None, :], BF16)

    def tok(col, rev):
        return pl.BlockSpec((nb, c, w), lambda b, ci: (b, nc - 1 - ci if rev else ci, col))

    vec = pl.BlockSpec((1, w), lambda b, ci: (0, 0))
    shape = jax.ShapeDtypeStruct((batch, seq, w), F32)
    state = pltpu.VMEM((nb, RWKV_HEADS // RWKV_GROUP_HEADS, gw, gw), F32)
    return pl.pallas_call(
        _rwkv_chunk_kernel,
        grid=(batch // nb, nc),
        in_specs=[
            tok(0, False), tok(1, False), tok(2, False), tok(0, False), tok(2, False),
            tok(0, True), tok(1, True), tok(2, True), tok(1, True), tok(3, True),
            vec, vec, pl.BlockSpec((w, w), lambda b, ci: (0, 0)),
        ],
        out_specs=[tok(0, False), tok(0, True)],
        out_shape=[shape, shape],
        scratch_shapes=[state, state],
        compiler_params=_params("parallel", "arbitrary"),
        name="rwkv_scan",
    )(rkv, rkv, rkv, lwa, lwa, rkv, rkv, rkv, lwa, lwa, kk_scale, k_a, ones_bd)


def _rwkv_post_kernel(yf_ref, yb_ref, r_ref, k_ref, v_ref, a0_ref, a1_ref, g_ref, ka_ref, rk_ref, lnw_ref, lnb_ref, o_ref):
    n = RWKV_N
    y = yf_ref[...] + yb_ref[...]
    k, v, ka = k_ref[...], v_ref[...], ka_ref[...]
    kd = k * (1.0 + (a0_ref[...] - 1.0) * ka) + k * (1.0 + (a1_ref[...] - 1.0) * ka)
    t = r_ref[...] * kd * rk_ref[...]
    lnw, lnb = lnw_ref[...], lnb_ref[...]
    outs = []
    for h in range(RWKV_HEADS):
        sl = slice(h * n, (h + 1) * n)
        yh = y[:, sl]
        mu = jnp.mean(yh, axis=-1, keepdims=True)
        yc = yh - mu
        var = jnp.mean(yc * yc, axis=-1, keepdims=True)
        bonus = jnp.sum(t[:, sl], axis=-1, keepdims=True)
        outs.append(yc * lax.rsqrt(var + RWKV_NORM_EPS) * lnw[:, sl] + lnb[:, sl] + bonus * v[:, sl])
    o_ref[...] = (jnp.concatenate(outs, axis=1) * g_ref[...]).astype(o_ref.dtype)


def _rwkv_post(yf, yb, rkv, lwa, g, k_a, r_k, ln_w, ln_b):
    batch, seq, w = yf.shape
    m = batch * seq
    tm = ROW_TILE
    rkv2 = rkv.reshape(m, 3 * w)
    lwa2 = lwa.reshape(m, 4 * w)
    tok = lambda col: pl.BlockSpec((tm, w), lambda i: (i, col))
    vec = pl.BlockSpec((1, w), lambda i: (0, 0))
    return pl.pallas_call(
        _rwkv_post_kernel,
        grid=(m // tm,),
        in_specs=[tok(0), tok(0), tok(0), tok(1), tok(2), tok(2), tok(3), tok(0), vec, vec, vec, vec],
        out_specs=pl.BlockSpec((tm, w), lambda i: (i, 0)),
        out_shape=jax.ShapeDtypeStruct((m, w), BF16),
        compiler_params=_params("parallel"),
        name="rwkv_post",
    )(yf.reshape(m, w), yb.reshape(m, w), rkv2, rkv2, rkv2, lwa2, lwa2, g.reshape(m, w), k_a, r_k, ln_w, ln_b)


def _merge_kernel(u_ref, ya_ref, yb_ref, yc_ref, ga_ref, gb_ref, gc_ref, wa_ref, wb_ref, wc_ref, o_ref):
    u = u_ref[...]
    acc = _sigmoid(_dot(u, ga_ref[...])) * _dot(ya_ref[...], wa_ref[...])
    acc += _sigmoid(_dot(u, gb_ref[...])) * _dot(yb_ref[...], wb_ref[...])
    acc += _sigmoid(_dot(u, gc_ref[...])) * _dot(yc_ref[...], wc_ref[...])
    o_ref[...] = acc.astype(o_ref.dtype)


def _merge(u, ya, yb, yc, w_gates, wa, wb, wc):
    m, d = u.shape
    tm, tn = MM_ROW_TILE, 512
    nj = d // tn
    act = lambda a: pl.BlockSpec((tm, a.shape[1]), lambda j, i: (i, 0))
    wsp = lambda a: pl.BlockSpec((a.shape[0], tn), lambda j, i: (0, j))
    gsp = lambda b: pl.BlockSpec((d, tn), lambda j, i: (0, b * nj + j))
    return pl.pallas_call(
        _merge_kernel,
        grid=(nj, m // tm),
        in_specs=[act(u), act(ya), act(yb), act(yc), gsp(0), gsp(1), gsp(2), wsp(wa), wsp(wb), wsp(wc)],
        out_specs=pl.BlockSpec((tm, tn), lambda j, i: (i, j)),
        out_shape=jax.ShapeDtypeStruct((m, d), BF16),
        compiler_params=_params("parallel", "arbitrary"),
        name="branch_merge",
    )(u, ya, yb, yc, w_gates, w_gates, w_gates, wa, wb, wc)


def _out_proj_kernel(x_ref, m_ref, w_ref, o_ref):
    o_ref[...] = x_ref[...] + _dot(m_ref[...], w_ref[...])


def _out_proj(x, merged, w):
    m, d = x.shape
    tm = ROW_TILE
    row = pl.BlockSpec((tm, d), lambda i: (i, 0))
    return pl.pallas_call(
        _out_proj_kernel,
        grid=(m // tm,),
        in_specs=[row, row, pl.BlockSpec((d, d), lambda i: (0, 0))],
        out_specs=row,
        out_shape=jax.ShapeDtypeStruct((m, d), F32),
        compiler_params=_params("parallel"),
        name="out_proj",
    )(x, merged, w)


def _block_diag2(w2):
    zero = jnp.zeros_like(w2[0])
    return jnp.concatenate([jnp.concatenate([w2[0], zero], axis=1), jnp.concatenate([zero, w2[1]], axis=1)], axis=0)


def _mixers(x, u, p, batch, seq):
    outs, lses = [], []
    for g in range(len(ATT_GROUPS)):
        qkv = _matmul(u, p["w_qkv"][g], 768, BF16)
        o, l = _attention_group(qkv, g, batch, seq)
        outs.append(o)
        lses.append(l)
    ya = _att_merge(outs, lses)

    rproj = _matmul(u, p["w_ret"], 1024, F32)
    ret_f, ret_b = _retention(rproj, p["ret_log_g"], batch, seq)
    yb = _ret_post(ret_f, ret_b, rproj, p["ret_norm"])

    rkv, lwa, g = _rwkv_prep(u, p["w_rwkv"], p["rwkv_conv"], p["rwkv_w0"], p["rwkv_w2bd"], p["rwkv_a0"], p["rwkv_a2bd"],
                             p["rwkv_g2"], batch, seq)
    yf, yb_dir = _rwkv_scan(rkv, lwa, p["rwkv_k_k"], p["rwkv_k_a"], batch, seq)
    yc = _rwkv_post(yf, yb_dir, rkv, lwa, g, p["rwkv_k_a"], p["rwkv_r_k"], p["rwkv_ln_w"], p["rwkv_ln_b"])

    merged = _merge(u, ya, yb, yc, p["w_gates"], p["w_branch_a"], p["w_branch_b"], p["w_branch_c"])
    return _out_proj(x, merged, p["w_out"])


def kernel(x_prompt, x_sample, ffn1_norm, ffn1_w_gate, ffn1_w_up, ffn1_w_down, mix_norm, w_in, ret_decay_logit, ret_norm, rwkv_conv, rwkv_w0, rwkv_w2, rwkv_a0, rwkv_a2, rwkv_g2, rwkv_k_k, rwkv_k_a, rwkv_r_k, rwkv_ln_w, rwkv_ln_b, w_branch_a, w_branch_b, w_branch_c, w_out, ffn2_norm, ffn2_w_gate, ffn2_w_up, ffn2_w_down, final_norm):
    depth = ffn1_norm.shape[0]
    w = RWKV_WIDTH
    ng, aw = len(ATT_GROUPS), ATT_OUT_WIDTH
    att3 = 3 * ng * aw
    ret_w = 2 * RET_HEADS * RET_DK + 2 * RET_HEADS * RET_DV
    layers = []
    for l in range(depth):
        w_in_l = w_in[l].astype(BF16)
        o1, o2, o3 = att3, att3 + ret_w, att3 + ret_w + RWKV_CONV_CH
        w_qkv = [jnp.concatenate([w_in_l[:, (t * ng + g) * aw:(t * ng + g + 1) * aw] for t in range(3)], axis=1)
                 for g in range(ng)]
        layers.append({
            "ffn1": (ffn1_norm[l], ffn1_w_gate[l].astype(BF16), ffn1_w_up[l].astype(BF16), ffn1_w_down[l].astype(BF16)),
            "ffn2": (ffn2_norm[l], ffn2_w_gate[l].astype(BF16), ffn2_w_up[l].astype(BF16), ffn2_w_down[l].astype(BF16)),
            "mix_norm": mix_norm[l],
            "w_qkv": w_qkv, "w_ret": w_in_l[:, o1:o2], "w_rwkv": w_in_l[:, o2:o3], "w_gates": w_in_l[:, o3:],
            "ret_log_g": jax.nn.log_sigmoid(ret_decay_logit[l].astype(F32)),
            "ret_norm": ret_norm[l].astype(F32),
            "rwkv_conv": rwkv_conv[l].astype(F32),
            "rwkv_w0": rwkv_w0[l].astype(F32).reshape(1, 2 * w),
            "rwkv_w2bd": _block_diag2(rwkv_w2[l]).astype(BF16),
            "rwkv_a0": rwkv_a0[l].astype(F32).reshape(1, 2 * w),
            "rwkv_a2bd": _block_diag2(rwkv_a2[l]).astype(BF16),
            "rwkv_g2": rwkv_g2[l].astype(BF16),
            "rwkv_k_k": rwkv_k_k[l].astype(F32).reshape(1, w),
            "rwkv_k_a": rwkv_k_a[l].astype(F32).reshape(1, w),
            "rwkv_r_k": rwkv_r_k[l].astype(F32).reshape(1, w),
            "rwkv_ln_w": rwkv_ln_w[l].astype(F32).reshape(1, w),
            "rwkv_ln_b": rwkv_ln_b[l].astype(F32).reshape(1, w),
            "w_branch_a": w_branch_a[l].astype(BF16), "w_branch_b": w_branch_b[l].astype(BF16),
            "w_branch_c": w_branch_c[l].astype(BF16), "w_out": w_out[l].astype(BF16),
        })

    def run(x):
        batch, seq, d = x.shape
        y = x.reshape(batch * seq, d)
        for l, p in enumerate(layers):
            y, u = _ffn(y, *p["ffn1"], p["mix_norm"], "next_norm")
            y = _mixers(y, u, p, batch, seq)
            y, = _ffn(y, *p["ffn2"], final_norm, "final" if l == depth - 1 else "plain")
        return y.reshape(batch, seq, d)

    return (run(x_prompt), run(x_sample))
```

```python
import functools

import numpy as np
import jax
import jax.numpy as jnp
from jax import lax
from jax.experimental import pallas as pl
from jax.experimental.pallas import tpu as pltpu

F32 = jnp.float32
BF16 = jnp.bfloat16

NORM_EPS = 1e-6

ATT_GROUPS = ((128, 1), (512, 4), (2048, 16))
ATT_HEADS = 4
ATT_HEAD_DIM = 128
ATT_OUT_WIDTH = ATT_HEADS * ATT_HEAD_DIM
ATT_HALF = 64
ATT_TQ = 128
ATT_BLOCK = 512

RET_HEADS = 4
RET_DK = 128
RET_DV = 256
RET_CHUNK = 128
RET_NORM_EPS = 1e-5

RWKV_HEADS = 8
RWKV_N = 64
RWKV_WIDTH = RWKV_HEADS * RWKV_N
RWKV_CONV_CH = 1920
RWKV_NORM_EPS = 64e-5
RWKV_CHUNK = 64
RWKV_GROUP_HEADS = 2
RWKV_BATCH_ROWS = 2

ROW_TILE = 512
MM_ROW_TILE = 1024
VMEM_LIMIT_BYTES = 56 * 1024 * 1024
NEG_BIG = -1e30


def _params(*sem):
    return pltpu.CompilerParams(dimension_semantics=sem, vmem_limit_bytes=VMEM_LIMIT_BYTES)


def _dot(a, b):
    return jnp.dot(a, b, preferred_element_type=F32)


def _dot_nt(a, b):
    return lax.dot_general(a, b, (((1,), (1,)), ((), ())), preferred_element_type=F32)


def _dot_tn(a, b):
    return lax.dot_general(a, b, (((0,), (0,)), ((), ())), preferred_element_type=F32)


def _split2(a):
    hi = a.astype(BF16)
    lo = (a - hi.astype(F32)).astype(BF16)
    return hi, lo


def _rms(x, gain):
    ms = jnp.mean(x * x, axis=-1, keepdims=True)
    return x * lax.rsqrt(ms + NORM_EPS) * gain


def _sigmoid(x):
    return 1.0 / (1.0 + jnp.exp(-x))


def _softplus(x):
    return jnp.maximum(x, 0.0) + jnp.log(1.0 + jnp.exp(-jnp.abs(x)))


def _ffn_kernel(x_ref, g_ref, wg_ref, wu_ref, wd_ref, g2_ref, o_ref, *rest, nf, mode):
    xn_ref = rest[-1]
    f = pl.program_id(1)

    @pl.when(f == 0)
    def _():
        xn_ref[...] = _rms(x_ref[...], g_ref[...]).astype(BF16)
        o_ref[...] = jnp.zeros_like(o_ref)

    xn = xn_ref[...]
    hg = _dot(xn, wg_ref[...])
    hu = _dot(xn, wu_ref[...])
    h = (hg * _sigmoid(hg) * hu).astype(BF16)
    o_ref[...] += _dot(h, wd_ref[...])

    @pl.when(f == nf - 1)
    def _():
        y = x_ref[...] + 0.5 * o_ref[...]
        if mode == "final":
            y = _rms(y, g2_ref[...])
        o_ref[...] = y
        if mode == "next_norm":
            rest[0][...] = _rms(y, g2_ref[...]).astype(BF16)


def _ffn(x, gain, wg, wu, wd, gain2, mode):
    m, d = x.shape
    ff = wg.shape[1]
    tm, tf = (ROW_TILE, 512) if mode == "next_norm" else (MM_ROW_TILE, 256)
    nf = ff // tf
    row = pl.BlockSpec((tm, d), lambda i, f: (i, 0))
    vec = pl.BlockSpec((1, d), lambda i, f: (0, 0))
    out_specs, out_shape = [row], [jax.ShapeDtypeStruct((m, d), F32)]
    if mode == "next_norm":
        out_specs.append(row)
        out_shape.append(jax.ShapeDtypeStruct((m, d), BF16))
    return pl.pallas_call(
        functools.partial(_ffn_kernel, nf=nf, mode=mode),
        grid=(m // tm, nf),
        in_specs=[row, vec,
                  pl.BlockSpec((d, tf), lambda i, f: (0, f)),
                  pl.BlockSpec((d, tf), lambda i, f: (0, f)),
                  pl.BlockSpec((tf, d), lambda i, f: (f, 0)),
                  vec],
        out_specs=out_specs,
        out_shape=out_shape,
        scratch_shapes=[pltpu.VMEM((tm, d), BF16)],
        compiler_params=_params("parallel", "arbitrary"),
        name="ffn",
    )(x, gain.reshape(1, d), wg, wu, wd, gain2.reshape(1, d))


def _matmul_kernel(x_ref, w_ref, o_ref):
    o_ref[...] = _dot(x_ref[...], w_ref[...]).astype(o_ref.dtype)


def _matmul(x, w, tn, out_dtype):
    m, d = x.shape
    n = w.shape[1]
    tm = MM_ROW_TILE
    return pl.pallas_call(
        _matmul_kernel,
        grid=(m // tm, n // tn),
        in_specs=[pl.BlockSpec((tm, d), lambda i, j: (i, 0)), pl.BlockSpec((d, tn), lambda i, j: (0, j))],
        out_specs=pl.BlockSpec((tm, tn), lambda i, j: (i, j)),
        out_shape=jax.ShapeDtypeStruct((m, n), out_dtype),
        compiler_params=_params("parallel", "arbitrary"),
        name="proj",
    )(x, w)


def _attn_kernel(q_ref, kp_ref, kc_ref, kn_ref, vp_ref, vc_ref, vn_ref, o_ref, l_ref, *, dil, slopes, sub_len):
    tq, half, dh = ATT_TQ, ATT_HALF, ATT_HEAD_DIM
    tqb = q_ref.shape[0]
    i, phase = pl.program_id(1), pl.program_id(2)
    out_rows = lambda j: pl.ds(j * tq * dil + phase, tq, stride=dil) if dil > 1 else pl.ds(j * tq, tq)
    qi = lax.broadcasted_iota(jnp.int32, (tq, tq + 2 * half), 0)
    kj = lax.broadcasted_iota(jnp.int32, (tq, tq + 2 * half), 1)
    rel = kj - half - qi
    band = jnp.where(jnp.abs(rel) <= half, 1, 0)
    dist = (dil * jnp.abs(rel)).astype(F32)
    scale = dh ** -0.5
    oks = []
    for j in range(tqb // tq):
        kidx = i * tqb + j * tq - half + kj
        oks.append(band * jnp.where(kidx >= 0, 1, 0) * jnp.where(kidx < sub_len, 1, 0) > 0)
    for h in range(ATT_HEADS):
        sl = slice(h * dh, (h + 1) * dh)
        kfull = jnp.concatenate([kp_ref[:, sl], kc_ref[:, sl], kn_ref[:, sl]], axis=0)
        vfull = jnp.concatenate([vp_ref[:, sl], vc_ref[:, sl], vn_ref[:, sl]], axis=0)
        nsub = tqb // tq
        rows = [slice(j * tq, (j + 1) * tq) for j in range(nsub)]
        keys = [slice(j * tq, (j + 1) * tq + 2 * half) for j in range(nsub)]
        scores = [_dot_nt(q_ref[rows[j], sl], kfull[keys[j]]) for j in range(nsub)]
        probs, dens = [], []
        for j in range(nsub):
            s = jnp.where(oks[j], scores[j] * scale - slopes[h] * dist, NEG_BIG)
            mx = jnp.max(s, axis=-1, keepdims=True)
            p = jnp.exp(s - mx)
            den = jnp.sum(p, axis=-1, keepdims=True)
            l_ref[h, out_rows(j), :] = jnp.broadcast_to(mx + jnp.log(den), (tq, dh))
            probs.append(p.astype(BF16))
            dens.append(den)
        for j in range(nsub):
            o_ref[h, out_rows(j), :] = _dot(probs[j], vfull[keys[j]]) / dens[j]


def _attention_group(qkv, g, batch, seq):
    dil = ATT_GROUPS[g][1]
    sub_len = seq // dil
    tqb = min(ATT_BLOCK, sub_len)
    nq = sub_len // tqb
    half = ATT_HALF
    w = ATT_OUT_WIDTH
    n = len(ATT_GROUPS) * ATT_HEADS
    slopes_all = np.power(np.float32(2.0), -8.0 * np.arange(1, n + 1, dtype=np.float32) / n)
    slopes = tuple(float(s) for s in slopes_all[g * ATT_HEADS:(g + 1) * ATT_HEADS])
    x = qkv.reshape(batch, sub_len, dil * 3 * w)
    per, last = tqb // half, sub_len // half - 1

    def spec(col, where):
        if where == 0:
            return pl.BlockSpec((None, tqb, w), lambda b, i, p: (b, i, p * 3 + col))
        if where < 0:
            return pl.BlockSpec((None, half, w), lambda b, i, p: (b, jnp.maximum(i * per - 1, 0), p * 3 + col))
        return pl.BlockSpec((None, half, w), lambda b, i, p: (b, jnp.minimum((i + 1) * per, last), p * 3 + col))

    nh, dh = ATT_HEADS, ATT_HEAD_DIM
    out_spec = pl.BlockSpec((nh, None, tqb * dil, dh), lambda b, i, p: (0, b, i, 0))
    out_shape = jax.ShapeDtypeStruct((nh, batch, seq, dh), F32)
    o, lse = pl.pallas_call(
        functools.partial(_attn_kernel, dil=dil, slopes=slopes, sub_len=sub_len),
        grid=(batch, nq, dil),
        in_specs=[spec(0, 0), spec(1, -1), spec(1, 0), spec(1, 1), spec(2, -1), spec(2, 0), spec(2, 1)],
        out_specs=[out_spec, out_spec],
        out_shape=[out_shape, out_shape],
        compiler_params=_params("parallel", "parallel", "arbitrary"),
        name=f"dilated_attn_g{g}",
    )(x, x, x, x, x, x, x)
    return o.reshape(nh, batch * seq, dh), lse.reshape(nh, batch * seq, dh)


def _att_merge_kernel(o0, o1, o2, l0, l1, l2, y_ref):
    dh = ATT_HEAD_DIM
    for h in range(ATT_HEADS):
        a, b, c = l0[h], l1[h], l2[h]
        mx = jnp.maximum(jnp.maximum(a, b), c)
        ea, eb, ec = jnp.exp(a - mx), jnp.exp(b - mx), jnp.exp(c - mx)
        y = (ea * o0[h] + eb * o1[h] + ec * o2[h]) / (ea + eb + ec)
        y_ref[:, h * dh:(h + 1) * dh] = y.astype(y_ref.dtype)


def _att_merge(outs, lses):
    nh, m, dh = outs[0].shape
    tm = ROW_TILE
    spec = pl.BlockSpec((nh, tm, dh), lambda i: (0, i, 0))
    return pl.pallas_call(
        _att_merge_kernel,
        grid=(m // tm,),
        in_specs=[spec] * 6,
        out_specs=pl.BlockSpec((tm, nh * dh), lambda i: (i, 0)),
        out_shape=jax.ShapeDtypeStruct((m, nh * dh), BF16),
        compiler_params=_params("parallel"),
        name="att_merge",
    )(*outs, *lses)


def _ret_kernel(lg_ref, qf_ref, kf_ref, vf_ref, qb_ref, kb_ref, vb_ref, of_ref, ob_ref, sf_ref, sb_ref):
    c, dk, dv = RET_CHUNK, RET_DK, RET_DV

    @pl.when(pl.program_id(1) == 0)
    def _():
        sf_ref[...] = jnp.zeros_like(sf_ref)
        sb_ref[...] = jnp.zeros_like(sb_ref)

    pos = lax.broadcasted_iota(jnp.int32, (c, 1), 0).astype(F32)
    diff = (lax.broadcasted_iota(jnp.int32, (c, c), 0) - lax.broadcasted_iota(jnp.int32, (c, c), 1)).astype(F32)
    up = jnp.maximum(diff, 0.0)
    dn = jnp.maximum(-diff, 0.0)
    for h in range(RET_HEADS):
        ks, vs = slice(h * dk, (h + 1) * dk), slice(h * dv, (h + 1) * dv)
        lgf = jnp.full((1, 1), lg_ref[0, h], F32)
        lgb = jnp.full((1, 1), lg_ref[1, h], F32)

        q = qf_ref[:, ks].astype(BF16)
        k = kf_ref[:, ks] * (dk ** -0.5)
        v = vf_ref[:, vs].astype(BF16)
        state = sf_ref[h]
        dmat = jnp.where(diff >= 0, jnp.exp(up * lgf), 0.0) + jnp.where(diff <= 0, jnp.exp(dn * lgb), 0.0)
        scores = _dot_nt(q, k.astype(BF16)) * dmat
        of_ref[:, vs] = _dot(scores.astype(BF16), v) + _dot(q, state.astype(BF16)) * jnp.exp((pos + 1.0) * lgf)
        kz = (k * jnp.exp((c - 1.0 - pos) * lgf)).astype(BF16)
        sf_ref[h] = jnp.exp(c * lgf) * state + _dot_tn(kz, v)

        q = qb_ref[:, ks].astype(BF16)
        k = kb_ref[:, ks] * (dk ** -0.5)
        v = vb_ref[:, vs].astype(BF16)
        state = sb_ref[h]
        ob_ref[:, vs] = _dot(q, state.astype(BF16)) * jnp.exp((c - pos) * lgb)
        kz = (k * jnp.exp(pos * lgb)).astype(BF16)
        sb_ref[h] = jnp.exp(c * lgb) * state + _dot_tn(kz, v)


def _retention(rproj, log_g, batch, seq):
    c = RET_CHUNK
    nc = seq // c
    qk_w, v_w = RET_HEADS * RET_DK, RET_HEADS * RET_DV
    x = rproj.reshape(batch, seq, rproj.shape[1])

    def tok(width, col, rev):
        return pl.BlockSpec((None, c, width), lambda b, ci: (b, nc - 1 - ci if rev else ci, col))

    shape = jax.ShapeDtypeStruct((batch, seq, v_w), F32)
    state = pltpu.VMEM((RET_HEADS, RET_DK, RET_DV), F32)
    v_col = 2 * qk_w // v_w
    return pl.pallas_call(
        _ret_kernel,
        grid=(batch, nc),
        in_specs=[pl.BlockSpec(memory_space=pltpu.SMEM),
                  tok(qk_w, 0, False), tok(qk_w, 1, False), tok(v_w, v_col, False),
                  tok(qk_w, 0, True), tok(qk_w, 1, True), tok(v_w, v_col, True)],
        out_specs=[tok(v_w, 0, False), tok(v_w, 0, True)],
        out_shape=[shape, shape],
        scratch_shapes=[state, state],
        compiler_params=_params("parallel", "arbitrary"),
        name="retention",
    )(log_g, x, x, x, x, x, x)


def _ret_post_kernel(rf_ref, rb_ref, g_ref, gain_ref, y_ref):
    ret = rf_ref[...] + rb_ref[...]
    gate = g_ref[...]
    gate = gate * _sigmoid(gate)
    gain = gain_ref[...]
    for h in range(RET_HEADS):
        sl = slice(h * RET_DV, (h + 1) * RET_DV)
        y = ret[:, sl]
        mu = jnp.mean(y, axis=-1, keepdims=True)
        yc = y - mu
        var = jnp.mean(yc * yc, axis=-1, keepdims=True)
        y_ref[:, sl] = (gate[:, sl] * yc * lax.rsqrt(var + RET_NORM_EPS) * gain[:, sl]).astype(y_ref.dtype)


def _ret_post(ret_f, ret_b, rproj, gain):
    batch, seq, w = ret_f.shape
    m = batch * seq
    tm = ROW_TILE
    gcol = rproj.shape[1] // w - 1
    row = pl.BlockSpec((tm, w), lambda i: (i, 0))
    return pl.pallas_call(
        _ret_post_kernel,
        grid=(m // tm,),
        in_specs=[row, row, pl.BlockSpec((tm, w), lambda i: (i, gcol)), pl.BlockSpec((1, w), lambda i: (0, 0))],
        out_specs=row,
        out_shape=jax.ShapeDtypeStruct((m, w), BF16),
        compiler_params=_params("parallel"),
        name="ret_post",
    )(ret_f.reshape(m, w), ret_b.reshape(m, w), rproj, gain.reshape(1, w))


def _rwkv_prep_kernel(u_ref, up_ref, un_ref, wf_ref, conv_ref, w0_ref, w2_ref, a0_ref, a2_ref, g2_ref,
                      rkv_ref, lwa_ref, g_ref, *, ns):
    i = pl.program_id(1)
    ts, halo = u_ref.shape[0], up_ref.shape[0]
    xa = _dot(jnp.concatenate([up_ref[...], u_ref[...], un_ref[...]], axis=0), wf_ref[...])
    x = xa[halo:halo + ts]
    w = RWKV_WIDTH
    row = lax.broadcasted_iota(jnp.int32, (ts, 1), 0)
    prev_row = xa[halo - 1:halo, :] * jnp.where(i > 0, 1.0, 0.0)
    next_row = xa[halo + ts:halo + ts + 1, :] * jnp.where(i < ns - 1, 1.0, 0.0)
    x_m1 = jnp.where(row == 0, prev_row, pltpu.roll(x, 1, 0))
    x_p1 = jnp.where(row == ts - 1, next_row, pltpu.roll(x, ts - 1, 0))
    cw = conv_ref[...]
    feats = cw[0:1, :] * x_m1 + cw[1:2, :] * x + cw[2:3, :] * x_p1
    rkv_ref[...] = feats[:, :3 * w]
    w_lo = jnp.tanh(feats[:, 3 * w:3 * w + 128]).astype(BF16)
    a_lo = feats[:, 3 * w + 128:3 * w + 256].astype(BF16)
    g_lo = _sigmoid(feats[:, 3 * w + 256:3 * w + 384]).astype(BF16)
    logw = -_softplus(-(w0_ref[...] + _dot(w_lo, w2_ref[...]))) - 0.5
    lwa_ref[:, :2 * w] = -jnp.exp(logw)
    lwa_ref[:, 2 * w:] = _sigmoid(a0_ref[...] + _dot(a_lo, a2_ref[...]))
    g_ref[...] = _dot(g_lo, g2_ref[...])


def _rwkv_prep(u, w_feat, conv, w0, w2bd, a0, a2bd, g2, batch, seq):
    ts = 512
    halo = 16
    ns = seq // ts
    d, ch = w_feat.shape
    w = RWKV_WIDTH
    x = u.reshape(batch, seq, d)
    full = lambda shape: pl.BlockSpec(shape, lambda b, i: (0, 0))
    return pl.pallas_call(
        functools.partial(_rwkv_prep_kernel, ns=ns),
        grid=(batch, ns),
        in_specs=[
            pl.BlockSpec((None, ts, d), lambda b, i: (b, i, 0)),
            pl.BlockSpec((None, halo, d), lambda b, i: (b, jnp.maximum(i * (ts // halo) - 1, 0), 0)),
            pl.BlockSpec((None, halo, d), lambda b, i: (b, jnp.minimum((i + 1) * (ts // halo), seq // halo - 1), 0)),
            full((d, ch)),
            full((3, ch)), full((1, 2 * w)), full((128, 2 * w)), full((1, 2 * w)), full((128, 2 * w)), full((128, w)),
        ],
        out_specs=[
            pl.BlockSpec((None, ts, 3 * w), lambda b, i: (b, i, 0)),
            pl.BlockSpec((None, ts, 4 * w), lambda b, i: (b, i, 0)),
            pl.BlockSpec((None, ts, w), lambda b, i: (b, i, 0)),
        ],
        out_shape=[
            jax.ShapeDtypeStruct((batch, seq, 3 * w), F32),
            jax.ShapeDtypeStruct((batch, seq, 4 * w), F32),
            jax.ShapeDtypeStruct((batch, seq, w), F32),
        ],
        compiler_params=_params("parallel", "arbitrary"),
        name="rwkv_prep",
    )(x, x, x, w_feat, conv, w0, w2bd, a0, a2bd, g2)


def _rwkv_fold_decay(r, k, v, lw, a, kk_scale, k_a, ones_bd, reverse):
    c = RWKV_CHUNK
    sgn = -1 if reverse else 1
    d = (lax.broadcasted_iota(jnp.int32, (c, c), 0) - lax.broadcasted_iota(jnp.int32, (c, c), 1)) * sgn
    tri = jnp.where(d >= 0, 1.0, 0.0).astype(BF16)
    hi = lw.astype(BF16)
    rem = lw - hi.astype(F32)
    mid = rem.astype(BF16)
    lo = (rem - mid.astype(F32)).astype(BF16)
    cs = _dot(tri, hi) + _dot(tri, mid) + _dot(tri, lo)
    g_all = jnp.exp(jnp.sum(lw, axis=0, keepdims=True))
    e_neg = jnp.exp(-cs)
    kks = k * kk_scale
    sq_hi, sq_lo = _split2(kks * kks)
    kk = kks * lax.rsqrt(_dot(sq_hi, ones_bd) + _dot(sq_lo, ones_bd) + 1e-12)
    abar_all = -kk * jnp.exp(cs - lw)
    bbar_all = kk * a * e_neg
    kbar_all = k * (1.0 + (a - 1.0) * k_a) * e_neg
    rbar_all = r * jnp.exp(cs)
    return abar_all, rbar_all, bbar_all, kbar_all, v, g_all


def _rwkv_masks(reverse):
    n, gw = RWKV_N, RWKV_GROUP_HEADS * RWKV_N
    row = lax.broadcasted_iota(jnp.int32, (gw, gw), 0)
    col = lax.broadcasted_iota(jnp.int32, (gw, gw), 1)
    same_head = (row // n) == (col // n)
    dd = jnp.where(same_head, (row - col) * (-1 if reverse else 1), -1)
    return same_head, dd > 0, dd >= 0, jnp.where(row == col, 1.0, 0.0)


def _rwkv_group_chain(folded, masks, g, s_ref):
    c, n, gh = RWKV_CHUNK, RWKV_N, RWKV_GROUP_HEADS
    gw = gh * n
    abar_all, rbar_all, bbar_all, kbar_all, v, g_all = folded
    same_head, strict, incl, eye = masks
    tile = lambda x: jnp.concatenate([x] * gh, axis=0)
    n_sq = c.bit_length() - 1
    gl = slice(g * gw, (g + 1) * gw)
    a_s = jnp.where(same_head, tile(abar_all[:, gl]), 0.0)
    r_s = jnp.where(same_head, tile(rbar_all[:, gl]), 0.0)
    v_s = jnp.where(same_head, tile(v[:, gl]), 0.0).astype(BF16)
    b_t = tile(bbar_all[:, gl].astype(BF16))
    k_t = tile(kbar_all[:, gl].astype(BF16))
    bk = jnp.concatenate([b_t, k_t], axis=0)
    f = _dot_nt(jnp.concatenate([a_s, r_s], axis=0).astype(BF16), bk)
    yield
    l_b = jnp.where(strict, f[:gw, :gw], 0.0)
    l_k = jnp.where(strict, f[:gw, gw:], 0.0).astype(BF16)
    m_b = jnp.where(incl, f[gw:, :gw], 0.0).astype(BF16)
    m_k = jnp.where(incl, f[gw:, gw:], 0.0).astype(BF16)
    tt = eye + l_b
    pwb = l_b.astype(BF16)
    pwb = _dot(pwb, pwb).astype(BF16)
    yield
    lkv = _dot(l_k, v_s).astype(BF16)
    yield
    for _ in range(n_sq - 2):
        both = _dot(pwb, jnp.concatenate([pwb, tt.astype(BF16)], axis=1))
        yield
        tt = tt + both[:, gw:]
        pwb = both[:, :gw].astype(BF16)
    tt = tt + _dot(pwb, tt.astype(BF16))
    yield
    zzb = _dot(tt.astype(BF16), jnp.concatenate([a_s.astype(BF16), lkv], axis=1)).astype(BF16)
    yield
    w_b = zzb[:, :gw]
    uv = jnp.concatenate([zzb[:, gw:], v_s], axis=0)
    qt = r_s + _dot(m_b, w_b)
    yield
    yloc = _dot(jnp.concatenate([m_b, m_k], axis=1), uv)
    yield
    state = s_ref[g]
    s_b = state.astype(BF16)
    y_s = _dot_nt(qt.astype(BF16), s_b) + yloc
    yield
    y = y_s[0:c]
    for i in range(1, gh):
        y = y + y_s[i * c:(i + 1) * c]
    gdec = g_all[:, gl]
    zb = _dot_tn(zzb, b_t)
    yield
    gmat = jnp.where(same_head, zb[gw:] + _dot_tn(v_s, k_t), 0.0) * gdec
    yield
    p_off = jnp.where(same_head, zb[:gw], 0.0) * gdec
    s_ref[g] = state * gdec + _dot(s_b, p_off.astype(BF16)) + gmat
    return y


def _rwkv_chunk_kernel(rf_ref, kf_ref, vf_ref, lwf_ref, af_ref, rb_ref, kb_ref, vb_ref, lwb_ref, ab_ref,
                       kk_ref, ka_ref, ones_ref, of_ref, ob_ref, sf_ref, sb_ref):
    @pl.when(pl.program_id(1) == 0)
    def _():
        sf_ref[...] = jnp.zeros_like(sf_ref)
        sb_ref[...] = jnp.zeros_like(sb_ref)

    kk_scale, k_a, ones_bd = kk_ref[...], ka_ref[...], ones_ref[...]
    masks = (_rwkv_masks(False), _rwkv_masks(True))
    ngroups = RWKV_HEADS // RWKV_GROUP_HEADS
    chains = []
    for b in range(rf_ref.shape[0]):
        fwd = _rwkv_fold_decay(rf_ref[b], kf_ref[b], vf_ref[b], lwf_ref[b], af_ref[b], kk_scale, k_a, ones_bd, False)
        bwd = _rwkv_fold_decay(rb_ref[b], kb_ref[b], vb_ref[b], lwb_ref[b], ab_ref[b], kk_scale, k_a, ones_bd, True)
        for g in range(ngroups):
            chains.append((of_ref, b, g, _rwkv_group_chain(fwd, masks[0], g, sf_ref.at[b])))
            chains.append((ob_ref, b, g, _rwkv_group_chain(bwd, masks[1], g, sb_ref.at[b])))
    gw = RWKV_GROUP_HEADS * RWKV_N
    while chains:
        live = []
        for o_ref, b, g, chain in chains:
            try:
                next(chain)
                live.append((o_ref, b, g, chain))
            except StopIteration as done:
                o_ref[b, :, g * gw:(g + 1) * gw] = done.value
        chains = live


def _rwkv_scan(rkv, lwa, kk_scale, k_a, batch, seq):
    c = RWKV_CHUNK
    nc = seq // c
    w = RWKV_WIDTH
    gw = RWKV_GROUP_HEADS * RWKV_N
    nb = RWKV_BATCH_ROWS
    head = np.arange(w) // RWKV_N
    ones_bd = jnp.asarray(head[:, None] == head[None, :], BF16)

    def tok(col, rev):
        return pl.BlockSpec((nb, c, w), lambda b, ci: (b, nc - 1 - ci if rev else ci, col))

    vec = pl.BlockSpec((1, w), lambda b, ci: (0, 0))
    shape = jax.ShapeDtypeStruct((batch, seq, w), F32)
    state = pltpu.VMEM((nb, RWKV_HEADS // RWKV_GROUP_HEADS, gw, gw), F32)
    return pl.pallas_call(
        _rwkv_chunk_kernel,
        grid=(batch // nb, nc),
        in_specs=[
            tok(0, False), tok(1, False), tok(2, False), tok(0, False), tok(2, False),
            tok(0, True), tok(1, True), tok(2, True), tok(1, True), tok(3, True),
            vec, vec, pl.BlockSpec((w, w), lambda b, ci: (0, 0)),
        ],
        out_specs=[tok(0, False), tok(0, True)],
        out_shape=[shape, shape],
        scratch_shapes=[state, state],
        compiler_params=_params("parallel", "arbitrary"),
        name="rwkv_scan",
    )(rkv, rkv, rkv, lwa, lwa, rkv, rkv, rkv, lwa, lwa, kk_scale, k_a, ones_bd)


def _rwkv_post_kernel(yf_ref, yb_ref, r_ref, k_ref, v_ref, a0_ref, a1_ref, g_ref, ka_ref, rk_ref, lnw_ref, lnb_ref, o_ref):
    n = RWKV_N
    y = yf_ref[...] + yb_ref[...]
    k, v, ka = k_ref[...], v_ref[...], ka_ref[...]
    kd = k * (1.0 + (a0_ref[...] - 1.0) * ka) + k * (1.0 + (a1_ref[...] - 1.0) * ka)
    t = r_ref[...] * kd * rk_ref[...]
    lnw, lnb = lnw_ref[...], lnb_ref[...]
    outs = []
    for h in range(RWKV_HEADS):
        sl = slice(h * n, (h + 1) * n)
        yh = y[:, sl]
        mu = jnp.mean(yh, axis=-1, keepdims=True)
        yc = yh - mu
        var = jnp.mean(yc * yc, axis=-1, keepdims=True)
        bonus = jnp.sum(t[:, sl], axis=-1, keepdims=True)
        outs.append(yc * lax.rsqrt(var + RWKV_NORM_EPS) * lnw[:, sl] + lnb[:, sl] + bonus * v[:, sl])
    o_ref[...] = (jnp.concatenate(outs, axis=1) * g_ref[...]).astype(o_ref.dtype)


def _rwkv_post(yf, yb, rkv, lwa, g, k_a, r_k, ln_w, ln_b):
    batch, seq, w = yf.shape
    m = batch * seq
    tm = ROW_TILE
    rkv2 = rkv.reshape(m, 3 * w)
    lwa2 = lwa.reshape(m, 4 * w)
    tok = lambda col: pl.BlockSpec((tm, w), lambda i: (i, col))
    vec = pl.BlockSpec((1, w), lambda i: (0, 0))
    return pl.pallas_call(
        _rwkv_post_kernel,
        grid=(m // tm,),
        in_specs=[tok(0), tok(0), tok(0), tok(1), tok(2), tok(2), tok(3), tok(0), vec, vec, vec, vec],
        out_specs=pl.BlockSpec((tm, w), lambda i: (i, 0)),
        out_shape=jax.ShapeDtypeStruct((m, w), BF16),
        compiler_params=_params("parallel"),
        name="rwkv_post",
    )(yf.reshape(m, w), yb.reshape(m, w), rkv2, rkv2, rkv2, lwa2, lwa2, g.reshape(m, w), k_a, r_k, ln_w, ln_b)


def _merge_kernel(u_ref, ya_ref, yb_ref, yc_ref, ga_ref, gb_ref, gc_ref, wa_ref, wb_ref, wc_ref, o_ref):
    u = u_ref[...]
    acc = _sigmoid(_dot(u, ga_ref[...])) * _dot(ya_ref[...], wa_ref[...])
    acc += _sigmoid(_dot(u, gb_ref[...])) * _dot(yb_ref[...], wb_ref[...])
    acc += _sigmoid(_dot(u, gc_ref[...])) * _dot(yc_ref[...], wc_ref[...])
    o_ref[...] = acc.astype(o_ref.dtype)


def _merge(u, ya, yb, yc, w_gates, wa, wb, wc):
    m, d = u.shape
    tm, tn = MM_ROW_TILE, 512
    nj = d // tn
    act = lambda a: pl.BlockSpec((tm, a.shape[1]), lambda j, i: (i, 0))
    wsp = lambda a: pl.BlockSpec((a.shape[0], tn), lambda j, i: (0, j))
    gsp = lambda b: pl.BlockSpec((d, tn), lambda j, i: (0, b * nj + j))
    return pl.pallas_call(
        _merge_kernel,
        grid=(nj, m // tm),
        in_specs=[act(u), act(ya), act(yb), act(yc), gsp(0), gsp(1), gsp(2), wsp(wa), wsp(wb), wsp(wc)],
        out_specs=pl.BlockSpec((tm, tn), lambda j, i: (i, j)),
        out_shape=jax.ShapeDtypeStruct((m, d), BF16),
        compiler_params=_params("parallel", "arbitrary"),
        name="branch_merge",
    )(u, ya, yb, yc, w_gates, w_gates, w_gates, wa, wb, wc)


def _out_proj_kernel(x_ref, m_ref, w_ref, o_ref):
    o_ref[...] = x_ref[...] + _dot(m_ref[...], w_ref[...])


def _out_proj(x, merged, w):
    m, d = x.shape
    tm = ROW_TILE
    row = pl.BlockSpec((tm, d), lambda i: (i, 0))
    return pl.pallas_call(
        _out_proj_kernel,
        grid=(m // tm,),
        in_specs=[row, row, pl.BlockSpec((d, d), lambda i: (0, 0))],
        out_specs=row,
        out_shape=jax.ShapeDtypeStruct((m, d), F32),
        compiler_params=_params("parallel"),
        name="out_proj",
    )(x, merged, w)


def _block_diag2(w2):
    zero = jnp.zeros_like(w2[0])
    return jnp.concatenate([jnp.concatenate([w2[0], zero], axis=1), jnp.concatenate([zero, w2[1]], axis=1)], axis=0)


def _mixers(x, u, p, batch, seq):
    outs, lses = [], []
    for g in range(len(ATT_GROUPS)):
        qkv = _matmul(u, p["w_qkv"][g], 768, BF16)
        o, l = _attention_group(qkv, g, batch, seq)
        outs.append(o)
        lses.append(l)
    ya = _att_merge(outs, lses)

    rproj = _matmul(u, p["w_ret"], 1024, F32)
    ret_f, ret_b = _retention(rproj, p["ret_log_g"], batch, seq)
    yb = _ret_post(ret_f, ret_b, rproj, p["ret_norm"])

    rkv, lwa, g = _rwkv_prep(u, p["w_rwkv"], p["rwkv_conv"], p["rwkv_w0"], p["rwkv_w2bd"], p["rwkv_a0"], p["rwkv_a2bd"],
                             p["rwkv_g2"], batch, seq)
    yf, yb_dir = _rwkv_scan(rkv, lwa, p["rwkv_k_k"], p["rwkv_k_a"], batch, seq)
    yc = _rwkv_post(yf, yb_dir, rkv, lwa, g, p["rwkv_k_a"], p["rwkv_r_k"], p["rwkv_ln_w"], p["rwkv_ln_b"])

    merged = _merge(u, ya, yb, yc, p["w_gates"], p["w_branch_a"], p["w_branch_b"], p["w_branch_c"])
    return _out_proj(x, merged, p["w_out"])


def kernel(x_prompt, x_sample, ffn1_norm, ffn1_w_gate, ffn1_w_up, ffn1_w_down, mix_norm, w_in, ret_decay_logit, ret_norm, rwkv_conv, rwkv_w0, rwkv_w2, rwkv_a0, rwkv_a2, rwkv_g2, rwkv_k_k, rwkv_k_a, rwkv_r_k, rwkv_ln_w, rwkv_ln_b, w_branch_a, w_branch_b, w_branch_c, w_out, ffn2_norm, ffn2_w_gate, ffn2_w_up, ffn2_w_down, final_norm):
    depth = ffn1_norm.shape[0]
    w = RWKV_WIDTH
    ng, aw = len(ATT_GROUPS), ATT_OUT_WIDTH
    att3 = 3 * ng * aw
    ret_w = 2 * RET_HEADS * RET_DK + 2 * RET_HEADS * RET_DV
    layers = []
    for l in range(depth):
        w_in_l = w_in[l].astype(BF16)
        o1, o2, o3 = att3, att3 + ret_w, att3 + ret_w + RWKV_CONV_CH
        w_qkv = [jnp.concatenate([w_in_l[:, (t * ng + g) * aw:(t * ng + g + 1) * aw] for t in range(3)], axis=1)
                 for g in range(ng)]
        layers.append({
            "ffn1": (ffn1_norm[l], ffn1_w_gate[l].astype(BF16), ffn1_w_up[l].astype(BF16), ffn1_w_down[l].astype(BF16)),
            "ffn2": (ffn2_norm[l], ffn2_w_gate[l].astype(BF16), ffn2_w_up[l].astype(BF16), ffn2_w_down[l].astype(BF16)),
            "mix_norm": mix_norm[l],
            "w_qkv": w_qkv, "w_ret": w_in_l[:, o1:o2], "w_rwkv": w_in_l[:, o2:o3], "w_gates": w_in_l[:, o3:],
            "ret_log_g": jax.nn.log_sigmoid(ret_decay_logit[l].astype(F32)),
            "ret_norm": ret_norm[l].astype(F32),
            "rwkv_conv": rwkv_conv[l].astype(F32),
            "rwkv_w0": rwkv_w0[l].astype(F32).reshape(1, 2 * w),
            "rwkv_w2bd": _block_diag2(rwkv_w2[l]).astype(BF16),
            "rwkv_a0": rwkv_a0[l].astype(F32).reshape(1, 2 * w),
            "rwkv_a2bd": _block_diag2(rwkv_a2[l]).astype(BF16),
            "rwkv_g2": rwkv_g2[l].astype(BF16),
            "rwkv_k_k": rwkv_k_k[l].astype(F32).reshape(1, w),
            "rwkv_k_a": rwkv_k_a[l].astype(F32).reshape(1, w),
            "rwkv_r_k": rwkv_r_k[l].astype(F32).reshape(1, w),
            "rwkv_ln_w": rwkv_ln_w[l].astype(F32).reshape(1, w),
            "rwkv_ln_b": rwkv_ln_b[l].astype(F32).reshape(1, w),
            "w_branch_a": w_branch_a[l].astype(BF16), "w_branch_b": w_branch_b[l].astype(BF16),
            "w_branch_c": w_branch_c[l].astype(BF16), "w_out": w_out[l].astype(BF16),
        })

    def run(x):
        batch, seq, d = x.shape
        y = x.reshape(batch * seq, d)
        for l, p in enumerate(layers):
            y, u = _ffn(y, *p["ffn1"], p["mix_norm"], "next_norm")
            y = _mixers(y, u, p, batch, seq)
            y, = _ffn(y, *p["ffn2"], final_norm, "final" if l == depth - 1 else "plain")
        return y.reshape(batch, seq, d)

    return (run(x_prompt), run(x_sample))
```

```python
import functools
import math

import numpy as np
import jax
import jax.numpy as jnp
from jax import lax
from jax.experimental import pallas as pl
from jax.experimental.pallas import tpu as pltpu

F32 = jnp.float32
BF16 = jnp.bfloat16

NORM_EPS = 1e-6

ATT_GROUPS = ((128, 1), (512, 4), (2048, 16))
ATT_HEADS = 4
ATT_HEAD_DIM = 128
ATT_OUT_WIDTH = ATT_HEADS * ATT_HEAD_DIM
ATT_HALF = 64
ATT_TQ = 128
ATT_BLOCK = 512

RET_HEADS = 4
RET_DK = 128
RET_DV = 256
RET_CHUNK = 256
RET_NORM_EPS = 1e-5

RWKV_HEADS = 8
RWKV_N = 64
RWKV_WIDTH = RWKV_HEADS * RWKV_N
RWKV_CONV_CH = 1920
RWKV_NORM_EPS = 64e-5
RWKV_CHUNK = 64
RWKV_GROUP_HEADS = 2
RWKV_BATCH_ROWS = 2

ROW_TILE = 512
MM_ROW_TILE = 1024
VMEM_LIMIT_BYTES = 56 * 1024 * 1024
NEG_BIG = -1e30


def _params(*sem):
    return pltpu.CompilerParams(dimension_semantics=sem, vmem_limit_bytes=VMEM_LIMIT_BYTES)


def _dot(a, b):
    return jnp.dot(a, b, preferred_element_type=F32)


def _dot_nt(a, b):
    return lax.dot_general(a, b, (((1,), (1,)), ((), ())), preferred_element_type=F32)


def _dot_tn(a, b):
    return lax.dot_general(a, b, (((0,), (0,)), ((), ())), preferred_element_type=F32)


def _split2(a):
    hi = a.astype(BF16)
    lo = (a - hi.astype(F32)).astype(BF16)
    return hi, lo


def _rms(x, gain):
    ms = jnp.mean(x * x, axis=-1, keepdims=True)
    return x * lax.rsqrt(ms + NORM_EPS) * gain


def _sigmoid(x):
    return 1.0 / (1.0 + jnp.exp(-x))


def _softplus(x):
    return jnp.maximum(x, 0.0) + jnp.log(1.0 + jnp.exp(-jnp.abs(x)))


def _ffn_kernel(x_ref, g_ref, wg_ref, wu_ref, wd_ref, g2_ref, o_ref, *rest, nf, mode):
    xn_ref = rest[-1]
    f = pl.program_id(1)

    @pl.when(f == 0)
    def _():
        xn_ref[...] = _rms(x_ref[...], g_ref[...]).astype(BF16)
        o_ref[...] = jnp.zeros_like(o_ref)

    xn = xn_ref[...]
    hg = _dot(xn, wg_ref[...])
    hu = _dot(xn, wu_ref[...])
    h = (hg * _sigmoid(hg) * hu).astype(BF16)
    o_ref[...] += _dot(h, wd_ref[...])

    @pl.when(f == nf - 1)
    def _():
        y = x_ref[...] + 0.5 * o_ref[...]
        if mode == "final":
            y = _rms(y, g2_ref[...])
        o_ref[...] = y
        if mode == "next_norm":
            rest[0][...] = _rms(y, g2_ref[...]).astype(BF16)


def _ffn(x, gain, wg, wu, wd, gain2, mode):
    m, d = x.shape
    ff = wg.shape[1]
    tm, tf = (ROW_TILE, 512) if mode == "next_norm" else (MM_ROW_TILE, 256)
    nf = ff // tf
    row = pl.BlockSpec((tm, d), lambda i, f: (i, 0))
    vec = pl.BlockSpec((1, d), lambda i, f: (0, 0))
    out_specs, out_shape = [row], [jax.ShapeDtypeStruct((m, d), F32)]
    if mode == "next_norm":
        out_specs.append(row)
        out_shape.append(jax.ShapeDtypeStruct((m, d), BF16))
    return pl.pallas_call(
        functools.partial(_ffn_kernel, nf=nf, mode=mode),
        grid=(m // tm, nf),
        in_specs=[row, vec,
                  pl.BlockSpec((d, tf), lambda i, f: (0, f)),
                  pl.BlockSpec((d, tf), lambda i, f: (0, f)),
                  pl.BlockSpec((tf, d), lambda i, f: (f, 0)),
                  vec],
        out_specs=out_specs,
        out_shape=out_shape,
        scratch_shapes=[pltpu.VMEM((tm, d), BF16)],
        compiler_params=_params("parallel", "arbitrary"),
        name="ffn",
    )(x, gain.reshape(1, d), wg, wu, wd, gain2.reshape(1, d))


def _matmul_kernel(x_ref, w_ref, o_ref):
    o_ref[...] = _dot(x_ref[...], w_ref[...]).astype(o_ref.dtype)


def _matmul(x, w, n, tn, out_dtype, first_block, block_stride):
    m, d = x.shape
    tm = MM_ROW_TILE
    return pl.pallas_call(
        _matmul_kernel,
        grid=(m // tm, n // tn),
        in_specs=[pl.BlockSpec((tm, d), lambda i, j: (i, 0)),
                  pl.BlockSpec((d, tn), lambda i, j: (0, first_block + j * block_stride))],
        out_specs=pl.BlockSpec((tm, tn), lambda i, j: (i, j)),
        out_shape=jax.ShapeDtypeStruct((m, n), out_dtype),
        compiler_params=_params("parallel", "arbitrary"),
        name="proj",
    )(x, w)


def _attn_kernel(q_ref, kp_ref, kc_ref, kn_ref, vp_ref, vc_ref, vn_ref, o_ref, l_ref, *, dil, slopes, sub_len):
    tq, half, dh = ATT_TQ, ATT_HALF, ATT_HEAD_DIM
    tqb = q_ref.shape[0]
    i, phase = pl.program_id(1), pl.program_id(2)
    out_rows = lambda j: pl.ds(j * tq * dil + phase, tq, stride=dil) if dil > 1 else pl.ds(j * tq, tq)
    qi = lax.broadcasted_iota(jnp.int32, (tq, tq + 2 * half), 0)
    kj = lax.broadcasted_iota(jnp.int32, (tq, tq + 2 * half), 1)
    rel = kj - half - qi
    band = jnp.where(jnp.abs(rel) <= half, 1, 0)
    dist = (dil * jnp.abs(rel)).astype(F32)
    scale = dh ** -0.5
    oks = []
    for j in range(tqb // tq):
        kidx = i * tqb + j * tq - half + kj
        oks.append(band * jnp.where(kidx >= 0, 1, 0) * jnp.where(kidx < sub_len, 1, 0) > 0)
    for h in range(ATT_HEADS):
        sl = slice(h * dh, (h + 1) * dh)
        kfull = jnp.concatenate([kp_ref[:, sl], kc_ref[:, sl], kn_ref[:, sl]], axis=0)
        vfull = jnp.concatenate([vp_ref[:, sl], vc_ref[:, sl], vn_ref[:, sl]], axis=0)
        nsub = tqb // tq
        rows = [slice(j * tq, (j + 1) * tq) for j in range(nsub)]
        keys = [slice(j * tq, (j + 1) * tq + 2 * half) for j in range(nsub)]
        scores = [_dot_nt(q_ref[rows[j], sl], kfull[keys[j]]) for j in range(nsub)]
        probs, dens = [], []
        for j in range(nsub):
            s = jnp.where(oks[j], scores[j] * scale - slopes[h] * dist, NEG_BIG)
            mx = jnp.max(s, axis=-1, keepdims=True)
            p = jnp.exp(s - mx)
            den = jnp.sum(p, axis=-1, keepdims=True)
            l_ref[h, out_rows(j), :] = jnp.broadcast_to(mx + jnp.log(den), (tq, dh))
            probs.append(p.astype(BF16))
            dens.append(den)
        for j in range(nsub):
            o_ref[h, out_rows(j), :] = _dot(probs[j], vfull[keys[j]]) / dens[j]


def _attention_group(qkv, g, batch, seq):
    dil = ATT_GROUPS[g][1]
    sub_len = seq // dil
    tqb = min(ATT_BLOCK, sub_len)
    nq = sub_len // tqb
    half = ATT_HALF
    w = ATT_OUT_WIDTH
    n = len(ATT_GROUPS) * ATT_HEADS
    slopes_all = np.power(np.float32(2.0), -8.0 * np.arange(1, n + 1, dtype=np.float32) / n)
    slopes = tuple(float(s) for s in slopes_all[g * ATT_HEADS:(g + 1) * ATT_HEADS])
    x = qkv.reshape(batch, sub_len, dil * 3 * w)
    per, last = tqb // half, sub_len // half - 1

    def spec(col, where):
        if where == 0:
            return pl.BlockSpec((None, tqb, w), lambda b, i, p: (b, i, p * 3 + col))
        if where < 0:
            return pl.BlockSpec((None, half, w), lambda b, i, p: (b, jnp.maximum(i * per - 1, 0), p * 3 + col))
        return pl.BlockSpec((None, half, w), lambda b, i, p: (b, jnp.minimum((i + 1) * per, last), p * 3 + col))

    nh, dh = ATT_HEADS, ATT_HEAD_DIM
    out_spec = pl.BlockSpec((nh, None, tqb * dil, dh), lambda b, i, p: (0, b, i, 0))
    out_shape = jax.ShapeDtypeStruct((nh, batch, seq, dh), F32)
    o, lse = pl.pallas_call(
        functools.partial(_attn_kernel, dil=dil, slopes=slopes, sub_len=sub_len),
        grid=(batch, nq, dil),
        in_specs=[spec(0, 0), spec(1, -1), spec(1, 0), spec(1, 1), spec(2, -1), spec(2, 0), spec(2, 1)],
        out_specs=[out_spec, out_spec],
        out_shape=[out_shape, out_shape],
        compiler_params=_params("parallel", "parallel", "arbitrary"),
        name=f"dilated_attn_g{g}",
    )(x, x, x, x, x, x, x)
    return o.reshape(nh, batch * seq, dh), lse.reshape(nh, batch * seq, dh)


def _att_merge_kernel(o0, o1, o2, l0, l1, l2, y_ref):
    dh = ATT_HEAD_DIM
    for h in range(ATT_HEADS):
        a, b, c = l0[h], l1[h], l2[h]
        mx = jnp.maximum(jnp.maximum(a, b), c)
        ea, eb, ec = jnp.exp(a - mx), jnp.exp(b - mx), jnp.exp(c - mx)
        y = (ea * o0[h] + eb * o1[h] + ec * o2[h]) / (ea + eb + ec)
        y_ref[:, h * dh:(h + 1) * dh] = y.astype(y_ref.dtype)


def _att_merge(outs, lses):
    nh, m, dh = outs[0].shape
    tm = ROW_TILE
    spec = pl.BlockSpec((nh, tm, dh), lambda i: (0, i, 0))
    return pl.pallas_call(
        _att_merge_kernel,
        grid=(m // tm,),
        in_specs=[spec] * 6,
        out_specs=pl.BlockSpec((tm, nh * dh), lambda i: (i, 0)),
        out_shape=jax.ShapeDtypeStruct((m, nh * dh), BF16),
        compiler_params=_params("parallel"),
        name="att_merge",
    )(*outs, *lses)


def _ret_kernel(lg_ref, qf_ref, kf_ref, vf_ref, qb_ref, kb_ref, vb_ref, of_ref, ob_ref, sf_ref, sb_ref):
    c, dk, dv = RET_CHUNK, RET_DK, RET_DV

    @pl.when(pl.program_id(1) == 0)
    def _():
        sf_ref[...] = jnp.zeros_like(sf_ref)
        sb_ref[...] = jnp.zeros_like(sb_ref)

    pos = lax.broadcasted_iota(jnp.int32, (c, 1), 0).astype(F32)
    diff = (lax.broadcasted_iota(jnp.int32, (c, c), 0) - lax.broadcasted_iota(jnp.int32, (c, c), 1)).astype(F32)
    up = jnp.maximum(diff, 0.0)
    dn = jnp.maximum(-diff, 0.0)
    heads = range(RET_HEADS)
    ks = [slice(h * dk, (h + 1) * dk) for h in heads]
    vs = [slice(h * dv, (h + 1) * dv) for h in heads]
    lgf = [jnp.full((1, 1), lg_ref[0, h], F32) for h in heads]
    lgb = [jnp.full((1, 1), lg_ref[1, h], F32) for h in heads]
    qf = [qf_ref[:, ks[h]].astype(BF16) for h in heads]
    kf = [kf_ref[:, ks[h]] * (dk ** -0.5) for h in heads]
    vf = [vf_ref[:, vs[h]].astype(BF16) for h in heads]
    qb = [qb_ref[:, ks[h]].astype(BF16) for h in heads]
    kb = [kb_ref[:, ks[h]] * (dk ** -0.5) for h in heads]
    vb = [vb_ref[:, vs[h]].astype(BF16) for h in heads]
    st_f = [sf_ref[h] for h in heads]
    st_b = [sb_ref[h] for h in heads]
    raw = [_dot_nt(qf[h], kf[h].astype(BF16)) for h in heads]
    cross_f = [_dot(qf[h], st_f[h].astype(BF16)) for h in heads]
    cross_b = [_dot(qb[h], st_b[h].astype(BF16)) for h in heads]
    scores = []
    for h in heads:
        dmat = jnp.where(diff >= 0, jnp.exp(up * lgf[h]), 0.0) + jnp.where(diff <= 0, jnp.exp(dn * lgb[h]), 0.0)
        scores.append((raw[h] * dmat).astype(BF16))
    for h in heads:
        of_ref[:, vs[h]] = _dot(scores[h], vf[h]) + cross_f[h] * jnp.exp((pos + 1.0) * lgf[h])
        ob_ref[:, vs[h]] = cross_b[h] * jnp.exp((c - pos) * lgb[h])
    for h in heads:
        kz = (kf[h] * jnp.exp((c - 1.0 - pos) * lgf[h])).astype(BF16)
        sf_ref[h] = jnp.exp(c * lgf[h]) * st_f[h] + _dot_tn(kz, vf[h])
        kz = (kb[h] * jnp.exp(pos * lgb[h])).astype(BF16)
        sb_ref[h] = jnp.exp(c * lgb[h]) * st_b[h] + _dot_tn(kz, vb[h])


def _retention(rproj, log_g, batch, seq):
    c = RET_CHUNK
    nc = seq // c
    qk_w, v_w = RET_HEADS * RET_DK, RET_HEADS * RET_DV
    x = rproj.reshape(batch, seq, rproj.shape[1])

    def tok(width, col, rev):
        return pl.BlockSpec((None, c, width), lambda b, ci: (b, nc - 1 - ci if rev else ci, col))

    shape = jax.ShapeDtypeStruct((batch, seq, v_w), F32)
    state = pltpu.VMEM((RET_HEADS, RET_DK, RET_DV), F32)
    v_col = 2 * qk_w // v_w
    return pl.pallas_call(
        _ret_kernel,
        grid=(batch, nc),
        in_specs=[pl.BlockSpec(memory_space=pltpu.SMEM),
                  tok(qk_w, 0, False), tok(qk_w, 1, False), tok(v_w, v_col, False),
                  tok(qk_w, 0, True), tok(qk_w, 1, True), tok(v_w, v_col, True)],
        out_specs=[tok(v_w, 0, False), tok(v_w, 0, True)],
        out_shape=[shape, shape],
        scratch_shapes=[state, state],
        compiler_params=_params("parallel", "arbitrary"),
        name="retention",
    )(log_g, x, x, x, x, x, x)


def _ret_post_kernel(rf_ref, rb_ref, g_ref, gain_ref, y_ref):
    ret = rf_ref[...] + rb_ref[...]
    gate = g_ref[...]
    gate = gate * _sigmoid(gate)
    gain = gain_ref[...]
    for h in range(RET_HEADS):
        sl = slice(h * RET_DV, (h + 1) * RET_DV)
        y = ret[:, sl]
        mu = jnp.mean(y, axis=-1, keepdims=True)
        yc = y - mu
        var = jnp.mean(yc * yc, axis=-1, keepdims=True)
        y_ref[:, sl] = (gate[:, sl] * yc * lax.rsqrt(var + RET_NORM_EPS) * gain[:, sl]).astype(y_ref.dtype)


def _ret_post(ret_f, ret_b, rproj, gain):
    batch, seq, w = ret_f.shape
    m = batch * seq
    tm = ROW_TILE
    gcol = rproj.shape[1] // w - 1
    row = pl.BlockSpec((tm, w), lambda i: (i, 0))
    return pl.pallas_call(
        _ret_post_kernel,
        grid=(m // tm,),
        in_specs=[row, row, pl.BlockSpec((tm, w), lambda i: (i, gcol)), pl.BlockSpec((1, w), lambda i: (0, 0))],
        out_specs=row,
        out_shape=jax.ShapeDtypeStruct((m, w), BF16),
        compiler_params=_params("parallel"),
        name="ret_post",
    )(ret_f.reshape(m, w), ret_b.reshape(m, w), rproj, gain.reshape(1, w))


def _rwkv_prep_kernel(u_ref, up_ref, un_ref, wf_ref, conv_ref, w0_ref, w2_ref, a0_ref, a2_ref, g2_ref,
                      rkv_ref, lwa_ref, g_ref, *, ns):
    i = pl.program_id(1)
    ts, halo = u_ref.shape[0], up_ref.shape[0]
    xa = _dot(jnp.concatenate([up_ref[...], u_ref[...], un_ref[...]], axis=0), wf_ref[...])
    x = xa[halo:halo + ts]
    w = RWKV_WIDTH
    row = lax.broadcasted_iota(jnp.int32, (ts, 1), 0)
    prev_row = xa[halo - 1:halo, :] * jnp.where(i > 0, 1.0, 0.0)
    next_row = xa[halo + ts:halo + ts + 1, :] * jnp.where(i < ns - 1, 1.0, 0.0)
    x_m1 = jnp.where(row == 0, prev_row, pltpu.roll(x, 1, 0))
    x_p1 = jnp.where(row == ts - 1, next_row, pltpu.roll(x, ts - 1, 0))
    cw = conv_ref[...]
    feats = cw[0:1, :] * x_m1 + cw[1:2, :] * x + cw[2:3, :] * x_p1
    rkv_ref[...] = feats[:, :3 * w]
    w_lo = jnp.tanh(feats[:, 3 * w:3 * w + 128]).astype(BF16)
    a_lo = feats[:, 3 * w + 128:3 * w + 256].astype(BF16)
    g_lo = _sigmoid(feats[:, 3 * w + 256:3 * w + 384]).astype(BF16)
    logw = -_softplus(-(w0_ref[...] + _dot(w_lo, w2_ref[...]))) - 0.5
    lwa_ref[:, :2 * w] = -jnp.exp(logw)
    lwa_ref[:, 2 * w:] = _sigmoid(a0_ref[...] + _dot(a_lo, a2_ref[...]))
    g_ref[...] = _dot(g_lo, g2_ref[...])


def _rwkv_prep(u, w_feat, feat_block, conv, w0, w2bd, a0, a2bd, g2, batch, seq):
    ts = 512
    halo = 16
    ns = seq // ts
    d, ch = w_feat.shape[0], RWKV_CONV_CH
    w = RWKV_WIDTH
    x = u.reshape(batch, seq, d)
    full = lambda shape: pl.BlockSpec(shape, lambda b, i: (0, 0))
    return pl.pallas_call(
        functools.partial(_rwkv_prep_kernel, ns=ns),
        grid=(batch, ns),
        in_specs=[
            pl.BlockSpec((None, ts, d), lambda b, i: (b, i, 0)),
            pl.BlockSpec((None, halo, d), lambda b, i: (b, jnp.maximum(i * (ts // halo) - 1, 0), 0)),
            pl.BlockSpec((None, halo, d), lambda b, i: (b, jnp.minimum((i + 1) * (ts // halo), seq // halo - 1), 0)),
            pl.BlockSpec((d, ch), lambda b, i: (0, feat_block)),
            full((3, ch)), full((1, 2 * w)), full((128, 2 * w)), full((1, 2 * w)), full((128, 2 * w)), full((128, w)),
        ],
        out_specs=[
            pl.BlockSpec((None, ts, 3 * w), lambda b, i: (b, i, 0)),
            pl.BlockSpec((None, ts, 4 * w), lambda b, i: (b, i, 0)),
            pl.BlockSpec((None, ts, w), lambda b, i: (b, i, 0)),
        ],
        out_shape=[
            jax.ShapeDtypeStruct((batch, seq, 3 * w), F32),
            jax.ShapeDtypeStruct((batch, seq, 4 * w), F32),
            jax.ShapeDtypeStruct((batch, seq, w), F32),
        ],
        compiler_params=_params("parallel", "arbitrary"),
        name="rwkv_prep",
    )(x, x, x, w_feat, conv, w0, w2bd, a0, a2bd, g2)


def _rwkv_fold_decay(r, k, v, lw, a, kk_scale, k_a, ones_bd, reverse):
    c = RWKV_CHUNK
    sgn = -1 if reverse else 1
    d = (lax.broadcasted_iota(jnp.int32, (c, c), 0) - lax.broadcasted_iota(jnp.int32, (c, c), 1)) * sgn
    tri = jnp.where(d >= 0, 1.0, 0.0).astype(BF16)
    hi = lw.astype(BF16)
    rem = lw - hi.astype(F32)
    mid = rem.astype(BF16)
    lo = (rem - mid.astype(F32)).astype(BF16)
    cs = _dot(tri, hi) + _dot(tri, mid) + _dot(tri, lo)
    g_all = jnp.exp(jnp.sum(lw, axis=0, keepdims=True))
    e_neg = jnp.exp(-cs)
    kks = k * kk_scale
    sq_hi, sq_lo = _split2(kks * kks)
    kk = kks * lax.rsqrt(_dot(sq_hi, ones_bd) + _dot(sq_lo, ones_bd) + 1e-12)
    abar_all = -kk * jnp.exp(cs - lw)
    bbar_all = kk * a * e_neg
    kbar_all = k * (1.0 + (a - 1.0) * k_a) * e_neg
    rbar_all = r * jnp.exp(cs)
    return abar_all, rbar_all, bbar_all, kbar_all, v, g_all


def _rwkv_masks(reverse):
    n, gw = RWKV_N, RWKV_GROUP_HEADS * RWKV_N
    row = lax.broadcasted_iota(jnp.int32, (gw, gw), 0)
    col = lax.broadcasted_iota(jnp.int32, (gw, gw), 1)
    same_head = (row // n) == (col // n)
    dd = jnp.where(same_head, (row - col) * (-1 if reverse else 1), -1)
    return same_head, dd > 0, dd >= 0, jnp.where(row == col, 1.0, 0.0)


def _rwkv_group_chain(folded, masks, g, s_ref):
    c, n, gh = RWKV_CHUNK, RWKV_N, RWKV_GROUP_HEADS
    gw = gh * n
    abar_all, rbar_all, bbar_all, kbar_all, v, g_all = folded
    same_head, strict, incl, eye = masks
    tile = lambda x: jnp.concatenate([x] * gh, axis=0)
    n_sq = c.bit_length() - 1
    gl = slice(g * gw, (g + 1) * gw)
    a_s = jnp.where(same_head, tile(abar_all[:, gl]), 0.0)
    r_s = jnp.where(same_head, tile(rbar_all[:, gl]), 0.0)
    v_s = jnp.where(same_head, tile(v[:, gl]), 0.0).astype(BF16)
    b_t = tile(bbar_all[:, gl].astype(BF16))
    k_t = tile(kbar_all[:, gl].astype(BF16))
    bk = jnp.concatenate([b_t, k_t], axis=0)
    f = _dot_nt(jnp.concatenate([a_s, r_s], axis=0).astype(BF16), bk)
    yield
    l_b = jnp.where(strict, f[:gw, :gw], 0.0)
    l_k = jnp.where(strict, f[:gw, gw:], 0.0).astype(BF16)
    m_b = jnp.where(incl, f[gw:, :gw], 0.0).astype(BF16)
    m_k = jnp.where(incl, f[gw:, gw:], 0.0).astype(BF16)
    tt = eye + l_b
    pwb = l_b.astype(BF16)
    pwb = _dot(pwb, pwb).astype(BF16)
    yield
    lkv = _dot(l_k, v_s).astype(BF16)
    yield
    for _ in range(n_sq - 2):
        both = _dot(pwb, jnp.concatenate([pwb, tt.astype(BF16)], axis=1))
        yield
        tt = tt + both[:, gw:]
        pwb = both[:, :gw].astype(BF16)
    tt = tt + _dot(pwb, tt.astype(BF16))
    yield
    zzb = _dot(tt.astype(BF16), jnp.concatenate([a_s.astype(BF16), lkv], axis=1)).astype(BF16)
    yield
    w_b = zzb[:, :gw]
    uv = jnp.concatenate([zzb[:, gw:], v_s], axis=0)
    qt = r_s + _dot(m_b, w_b)
    yield
    yloc = _dot(jnp.concatenate([m_b, m_k], axis=1), uv)
    yield
    state = s_ref[g]
    s_b = state.astype(BF16)
    y_s = _dot_nt(qt.astype(BF16), s_b) + yloc
    yield
    y = y_s[0:c]
    for i in range(1, gh):
        y = y + y_s[i * c:(i + 1) * c]
    gdec = g_all[:, gl]
    zb = _dot_tn(zzb, b_t)
    yield
    gmat = jnp.where(same_head, zb[gw:] + _dot_tn(v_s, k_t), 0.0) * gdec
    yield
    p_off = jnp.where(same_head, zb[:gw], 0.0) * gdec
    s_ref[g] = state * gdec + _dot(s_b, p_off.astype(BF16)) + gmat
    return y


def _rwkv_chunk_kernel(rf_ref, kf_ref, vf_ref, lwf_ref, af_ref, rb_ref, kb_ref, vb_ref, lwb_ref, ab_ref,
                       kk_ref, ka_ref, ones_ref, of_ref, ob_ref, sf_ref, sb_ref):
    @pl.when(pl.program_id(1) == 0)
    def _():
        sf_ref[...] = jnp.zeros_like(sf_ref)
        sb_ref[...] = jnp.zeros_like(sb_ref)

    kk_scale, k_a, ones_bd = kk_ref[...], ka_ref[...], ones_ref[...]
    masks = (_rwkv_masks(False), _rwkv_masks(True))
    ngroups = RWKV_HEADS // RWKV_GROUP_HEADS
    chains = []
    for b in range(rf_ref.shape[0]):
        fwd = _rwkv_fold_decay(rf_ref[b], kf_ref[b], vf_ref[b], lwf_ref[b], af_ref[b], kk_scale, k_a, ones_bd, False)
        bwd = _rwkv_fold_decay(rb_ref[b], kb_ref[b], vb_ref[b], lwb_ref[b], ab_ref[b], kk_scale, k_a, ones_bd, True)
        for g in range(ngroups):
            chains.append((of_ref, b, g, _rwkv_group_chain(fwd, masks[0], g, sf_ref.at[b])))
            chains.append((ob_ref, b, g, _rwkv_group_chain(bwd, masks[1], g, sb_ref.at[b])))
    gw = RWKV_GROUP_HEADS * RWKV_N
    while chains:
        live = []
        for o_ref, b, g, chain in chains:
            try:
                next(chain)
                live.append((o_ref, b, g, chain))
            except StopIteration as done:
                o_ref[b, :, g * gw:(g + 1) * gw] = done.value
        chains = live


def _rwkv_scan(rkv, lwa, kk_scale, k_a, batch, seq):
    c = RWKV_CHUNK
    nc = seq // c
    w = RWKV_WIDTH
    gw = RWKV_GROUP_HEADS * RWKV_N
    nb = math.gcd(batch, RWKV_BATCH_ROWS)
    head = np.arange(w) // RWKV_N
    ones_bd = jnp.asarray(head[:, None] == head[None, :], BF16)

    def tok(col, rev):
        return pl.BlockSpec((nb, c, w), lambda b, ci: (b, nc - 1 - ci if rev else ci, col))

    vec = pl.BlockSpec((1, w), lambda b, ci: (0, 0))
    shape = jax.ShapeDtypeStruct((batch, seq, w), F32)
    state = pltpu.VMEM((nb, RWKV_HEADS // RWKV_GROUP_HEADS, gw, gw), F32)
    return pl.pallas_call(
        _rwkv_chunk_kernel,
        grid=(batch // nb, nc),
        in_specs=[
            tok(0, False), tok(1, False), tok(2, False), tok(0, False), tok(2, False),
            tok(0, True), tok(1, True), tok(2, True), tok(1, True), tok(3, True),
            vec, vec, pl.BlockSpec((w, w), lambda b, ci: (0, 0)),
        ],
        out_specs=[tok(0, False), tok(0, True)],
        out_shape=[shape, shape],
        scratch_shapes=[state, state],
        compiler_params=_params("parallel", "arbitrary"),
        name="rwkv_scan",
    )(rkv, rkv, rkv, lwa, lwa, rkv, rkv, rkv, lwa, lwa, kk_scale, k_a, ones_bd)


def _rwkv_post_kernel(yf_ref, yb_ref, r_ref, k_ref, v_ref, a0_ref, a1_ref, g_ref, ka_ref, rk_ref, lnw_ref, lnb_ref, o_ref):
    n = RWKV_N
    y = yf_ref[...] + yb_ref[...]
    k, v, ka = k_ref[...], v_ref[...], ka_ref[...]
    kd = k * (1.0 + (a0_ref[...] - 1.0) * ka) + k * (1.0 + (a1_ref[...] - 1.0) * ka)
    t = r_ref[...] * kd * rk_ref[...]
    lnw, lnb = lnw_ref[...], lnb_ref[...]
    outs = []
    for h in range(RWKV_HEADS):
        sl = slice(h * n, (h + 1) * n)
        yh = y[:, sl]
        mu = jnp.mean(yh, axis=-1, keepdims=True)
        yc = yh - mu
        var = jnp.mean(yc * yc, axis=-1, keepdims=True)
        bonus = jnp.sum(t[:, sl], axis=-1, keepdims=True)
        outs.append(yc * lax.rsqrt(var + RWKV_NORM_EPS) * lnw[:, sl] + lnb[:, sl] + bonus * v[:, sl])
    o_ref[...] = (jnp.concatenate(outs, axis=1) * g_ref[...]).astype(o_ref.dtype)


def _rwkv_post(yf, yb, rkv, lwa, g, k_a, r_k, ln_w, ln_b):
    batch, seq, w = yf.shape
    m = batch * seq
    tm = ROW_TILE
    rkv2 = rkv.reshape(m, 3 * w)
    lwa2 = lwa.reshape(m, 4 * w)
    tok = lambda col: pl.BlockSpec((tm, w), lambda i: (i, col))
    vec = pl.BlockSpec((1, w), lambda i: (0, 0))
    return pl.pallas_call(
        _rwkv_post_kernel,
        grid=(m // tm,),
        in_specs=[tok(0), tok(0), tok(0), tok(1), tok(2), tok(2), tok(3), tok(0), vec, vec, vec, vec],
        out_specs=pl.BlockSpec((tm, w), lambda i: (i, 0)),
        out_shape=jax.ShapeDtypeStruct((m, w), BF16),
        compiler_params=_params("parallel"),
        name="rwkv_post",
    )(yf.reshape(m, w), yb.reshape(m, w), rkv2, rkv2, rkv2, lwa2, lwa2, g.reshape(m, w), k_a, r_k, ln_w, ln_b)


def _merge_kernel(u_ref, ya_ref, yb_ref, yc_ref, ga_ref, gb_ref, gc_ref, wa_ref, wb_ref, wc_ref, o_ref):
    u = u_ref[...]
    acc = _sigmoid(_dot(u, ga_ref[...])) * _dot(ya_ref[...], wa_ref[...])
    acc += _sigmoid(_dot(u, gb_ref[...])) * _dot(yb_ref[...], wb_ref[...])
    acc += _sigmoid(_dot(u, gc_ref[...])) * _dot(yc_ref[...], wc_ref[...])
    o_ref[...] = acc.astype(o_ref.dtype)


def _merge(u, ya, yb, yc, w_gates, wa, wb, wc):
    m, d = u.shape
    tm, tn = MM_ROW_TILE, 512
    nj = d // tn
    act = lambda a: pl.BlockSpec((tm, a.shape[1]), lambda j, i: (i, 0))
    wsp = lambda a: pl.BlockSpec((a.shape[0], tn), lambda j, i: (0, j))
    gsp = lambda b: pl.BlockSpec((d, tn), lambda j, i: (0, b * nj + j))
    return pl.pallas_call(
        _merge_kernel,
        grid=(nj, m // tm),
        in_specs=[act(u), act(ya), act(yb), act(yc), gsp(0), gsp(1), gsp(2), wsp(wa), wsp(wb), wsp(wc)],
        out_specs=pl.BlockSpec((tm, tn), lambda j, i: (i, j)),
        out_shape=jax.ShapeDtypeStruct((m, d), BF16),
        compiler_params=_params("parallel", "arbitrary"),
        name="branch_merge",
    )(u, ya, yb, yc, w_gates, w_gates, w_gates, wa, wb, wc)


def _out_proj_kernel(x_ref, m_ref, w_ref, o_ref):
    o_ref[...] = x_ref[...] + _dot(m_ref[...], w_ref[...])


def _out_proj(x, merged, w):
    m, d = x.shape
    tm = ROW_TILE
    row = pl.BlockSpec((tm, d), lambda i: (i, 0))
    return pl.pallas_call(
        _out_proj_kernel,
        grid=(m // tm,),
        in_specs=[row, row, pl.BlockSpec((d, d), lambda i: (0, 0))],
        out_specs=row,
        out_shape=jax.ShapeDtypeStruct((m, d), F32),
        compiler_params=_params("parallel"),
        name="out_proj",
    )(x, merged, w)


def _block_diag2(w2):
    zero = jnp.zeros_like(w2[0])
    return jnp.concatenate([jnp.concatenate([w2[0], zero], axis=1), jnp.concatenate([zero, w2[1]], axis=1)], axis=0)


def _mixers(x, u, p, batch, seq):
    ng, aw = len(ATT_GROUPS), ATT_OUT_WIDTH
    att3 = 3 * ng * aw
    ret_w = 2 * RET_HEADS * RET_DK + 2 * RET_HEADS * RET_DV
    outs, lses = [], []
    for g in range(ng):
        qkv = _matmul(u, p["w_in"], 3 * aw, aw, BF16, g, ng)
        o, l = _attention_group(qkv, g, batch, seq)
        outs.append(o)
        lses.append(l)
    ya = _att_merge(outs, lses)

    rproj = _matmul(u, p["w_in"], ret_w, aw, F32, att3 // aw, 1)
    ret_f, ret_b = _retention(rproj, p["ret_log_g"], batch, seq)
    yb = _ret_post(ret_f, ret_b, rproj, p["ret_norm"])

    rkv, lwa, g = _rwkv_prep(u, p["w_in"], (att3 + ret_w) // RWKV_CONV_CH, p["rwkv_conv"], p["rwkv_w0"], p["rwkv_w2bd"], p["rwkv_a0"], p["rwkv_a2bd"],
                             p["rwkv_g2"], batch, seq)
    yf, yb_dir = _rwkv_scan(rkv, lwa, p["rwkv_k_k"], p["rwkv_k_a"], batch, seq)
    yc = _rwkv_post(yf, yb_dir, rkv, lwa, g, p["rwkv_k_a"], p["rwkv_r_k"], p["rwkv_ln_w"], p["rwkv_ln_b"])

    merged = _merge(u, ya, yb, yc, p["w_gates"], p["w_branch_a"], p["w_branch_b"], p["w_branch_c"])
    return _out_proj(x, merged, p["w_out"])


def kernel(x_prompt, x_sample, ffn1_norm, ffn1_w_gate, ffn1_w_up, ffn1_w_down, mix_norm, w_in, ret_decay_logit, ret_norm, rwkv_conv, rwkv_w0, rwkv_w2, rwkv_a0, rwkv_a2, rwkv_g2, rwkv_k_k, rwkv_k_a, rwkv_r_k, rwkv_ln_w, rwkv_ln_b, w_branch_a, w_branch_b, w_branch_c, w_out, ffn2_norm, ffn2_w_gate, ffn2_w_up, ffn2_w_down, final_norm):
    depth = ffn1_norm.shape[0]
    w = RWKV_WIDTH
    gates0 = w_in.shape[2] - 3 * x_prompt.shape[2]
    layers = []
    for l in range(depth):
        w_in_l = w_in[l].astype(BF16)
        layers.append({
            "ffn1": (ffn1_norm[l], ffn1_w_gate[l].astype(BF16), ffn1_w_up[l].astype(BF16), ffn1_w_down[l].astype(BF16)),
            "ffn2": (ffn2_norm[l], ffn2_w_gate[l].astype(BF16), ffn2_w_up[l].astype(BF16), ffn2_w_down[l].astype(BF16)),
            "mix_norm": mix_norm[l],
            "w_in": w_in_l, "w_gates": w_in_l[:, gates0:],
            "ret_log_g": jax.nn.log_sigmoid(ret_decay_logit[l].astype(F32)),
            "ret_norm": ret_norm[l].astype(F32),
            "rwkv_conv": rwkv_conv[l].astype(F32),
            "rwkv_w0": rwkv_w0[l].astype(F32).reshape(1, 2 * w),
            "rwkv_w2bd": _block_diag2(rwkv_w2[l]).astype(BF16),
            "rwkv_a0": rwkv_a0[l].astype(F32).reshape(1, 2 * w),
            "rwkv_a2bd": _block_diag2(rwkv_a2[l]).astype(BF16),
            "rwkv_g2": rwkv_g2[l].astype(BF16),
            "rwkv_k_k": rwkv_k_k[l].astype(F32).reshape(1, w),
            "rwkv_k_a": rwkv_k_a[l].astype(F32).reshape(1, w),
            "rwkv_r_k": rwkv_r_k[l].astype(F32).reshape(1, w),
            "rwkv_ln_w": rwkv_ln_w[l].astype(F32).reshape(1, w),
            "rwkv_ln_b": rwkv_ln_b[l].astype(F32).reshape(1, w),
            "w_branch_a": w_branch_a[l].astype(BF16), "w_branch_b": w_branch_b[l].astype(BF16),
            "w_branch_c": w_branch_c[l].astype(BF16), "w_out": w_out[l].astype(BF16),
        })

    def run(x):
        batch, seq, d = x.shape
        y = x.reshape(batch * seq, d)
        for l, p in enumerate(layers):
            y, u = _ffn(y, *p["ffn1"], p["mix_norm"], "next_norm")
            y = _mixers(y, u, p, batch, seq)
            y, = _ffn(y, *p["ffn2"], final_norm, "final" if l == depth - 1 else "plain")
        return y.reshape(batch, seq, d)

    return (run(x_prompt), run(x_sample))
```

```python
import functools
import math

import numpy as np
import jax
import jax.numpy as jnp
from jax import lax
from jax.experimental import pallas as pl
from jax.experimental.pallas import tpu as pltpu

F32 = jnp.float32
BF16 = jnp.bfloat16

NORM_EPS = 1e-6

ATT_GROUPS = ((128, 1), (512, 4), (2048, 16))
ATT_HEADS = 4
ATT_HEAD_DIM = 128
ATT_OUT_WIDTH = ATT_HEADS * ATT_HEAD_DIM
ATT_HALF = 64
ATT_TQ = 128
ATT_BLOCK = 512

RET_HEADS = 4
RET_DK = 128
RET_DV = 256
RET_CHUNK = 256
RET_NORM_EPS = 1e-5

RWKV_HEADS = 8
RWKV_N = 64
RWKV_WIDTH = RWKV_HEADS * RWKV_N
RWKV_CONV_CH = 1920
RWKV_NORM_EPS = 64e-5
RWKV_CHUNK = 64
RWKV_GROUP_HEADS = 2
RWKV_BATCH_ROWS = 2

ROW_TILE = 512
MM_ROW_TILE = 1024
VMEM_LIMIT_BYTES = 56 * 1024 * 1024
NEG_BIG = -1e30


def _params(*sem):
    return pltpu.CompilerParams(dimension_semantics=sem, vmem_limit_bytes=VMEM_LIMIT_BYTES)


def _dot(a, b):
    return jnp.dot(a, b, preferred_element_type=F32)


def _dot_nt(a, b):
    return lax.dot_general(a, b, (((1,), (1,)), ((), ())), preferred_element_type=F32)


def _dot_tn(a, b):
    return lax.dot_general(a, b, (((0,), (0,)), ((), ())), preferred_element_type=F32)


def _split2(a):
    hi = a.astype(BF16)
    lo = (a - hi.astype(F32)).astype(BF16)
    return hi, lo


def _rms(x, gain):
    ms = jnp.mean(x * x, axis=-1, keepdims=True)
    return x * lax.rsqrt(ms + NORM_EPS) * gain


def _sigmoid(x):
    return 1.0 / (1.0 + jnp.exp(-x))


def _softplus(x):
    return jnp.maximum(x, 0.0) + jnp.log(1.0 + jnp.exp(-jnp.abs(x)))


def _ffn_kernel(x_ref, g_ref, wg_ref, wu_ref, wd_ref, g2_ref, o_ref, *rest, nf, mode):
    xn_ref = rest[-1]
    f = pl.program_id(1)

    @pl.when(f == 0)
    def _():
        xn_ref[...] = _rms(x_ref[...], g_ref[...]).astype(BF16)
        o_ref[...] = jnp.zeros_like(o_ref)

    xn = xn_ref[...]
    hg = _dot(xn, wg_ref[...])
    hu = _dot(xn, wu_ref[...])
    h = (hg * _sigmoid(hg) * hu).astype(BF16)
    o_ref[...] += _dot(h, wd_ref[...])

    @pl.when(f == nf - 1)
    def _():
        y = x_ref[...] + 0.5 * o_ref[...]
        if mode == "final":
            y = _rms(y, g2_ref[...])
        o_ref[...] = y
        if mode == "next_norm":
            rest[0][...] = _rms(y, g2_ref[...]).astype(BF16)


def _ffn(x, gain, wg, wu, wd, gain2, mode):
    m, d = x.shape
    ff = wg.shape[1]
    tm, tf = (ROW_TILE, 512) if mode == "next_norm" else (MM_ROW_TILE, 256)
    nf = ff // tf
    row = pl.BlockSpec((tm, d), lambda i, f: (i, 0))
    vec = pl.BlockSpec((1, d), lambda i, f: (0, 0))
    out_specs, out_shape = [row], [jax.ShapeDtypeStruct((m, d), F32)]
    if mode == "next_norm":
        out_specs.append(row)
        out_shape.append(jax.ShapeDtypeStruct((m, d), BF16))
    return pl.pallas_call(
        functools.partial(_ffn_kernel, nf=nf, mode=mode),
        grid=(m // tm, nf),
        in_specs=[row, vec,
                  pl.BlockSpec((d, tf), lambda i, f: (0, f)),
                  pl.BlockSpec((d, tf), lambda i, f: (0, f)),
                  pl.BlockSpec((tf, d), lambda i, f: (f, 0)),
                  vec],
        out_specs=out_specs,
        out_shape=out_shape,
        scratch_shapes=[pltpu.VMEM((tm, d), BF16)],
        compiler_params=_params("parallel", "arbitrary"),
        name="ffn",
    )(x, gain.reshape(1, d), wg, wu, wd, gain2.reshape(1, d))


def _matmul_kernel(x_ref, w_ref, o_ref):
    o_ref[...] = _dot(x_ref[...], w_ref[...]).astype(o_ref.dtype)


def _matmul(x, w, n, tn, out_dtype, first_block, block_stride):
    m, d = x.shape
    tm = MM_ROW_TILE
    return pl.pallas_call(
        _matmul_kernel,
        grid=(m // tm, n // tn),
        in_specs=[pl.BlockSpec((tm, d), lambda i, j: (i, 0)),
                  pl.BlockSpec((d, tn), lambda i, j: (0, first_block + j * block_stride))],
        out_specs=pl.BlockSpec((tm, tn), lambda i, j: (i, j)),
        out_shape=jax.ShapeDtypeStruct((m, n), out_dtype),
        compiler_params=_params("parallel", "arbitrary"),
        name="proj",
    )(x, w)


def _attn_kernel(q_ref, kp_ref, kc_ref, kn_ref, vp_ref, vc_ref, vn_ref, o_ref, l_ref, *, dil, slopes, sub_len):
    tq, half, dh = ATT_TQ, ATT_HALF, ATT_HEAD_DIM
    tqb = q_ref.shape[0]
    i, phase = pl.program_id(1), pl.program_id(2)
    out_rows = lambda j: pl.ds(j * tq * dil + phase, tq, stride=dil) if dil > 1 else pl.ds(j * tq, tq)
    qi = lax.broadcasted_iota(jnp.int32, (tq, tq + 2 * half), 0)
    kj = lax.broadcasted_iota(jnp.int32, (tq, tq + 2 * half), 1)
    rel = kj - half - qi
    band = jnp.where(jnp.abs(rel) <= half, 1, 0)
    dist = (dil * jnp.abs(rel)).astype(F32)
    scale = dh ** -0.5
    oks = []
    for j in range(tqb // tq):
        kidx = i * tqb + j * tq - half + kj
        oks.append(band * jnp.where(kidx >= 0, 1, 0) * jnp.where(kidx < sub_len, 1, 0) > 0)
    for h in range(ATT_HEADS):
        sl = slice(h * dh, (h + 1) * dh)
        kfull = jnp.concatenate([kp_ref[:, sl], kc_ref[:, sl], kn_ref[:, sl]], axis=0)
        vfull = jnp.concatenate([vp_ref[:, sl], vc_ref[:, sl], vn_ref[:, sl]], axis=0)
        nsub = tqb // tq
        rows = [slice(j * tq, (j + 1) * tq) for j in range(nsub)]
        keys = [slice(j * tq, (j + 1) * tq + 2 * half) for j in range(nsub)]
        scores = [_dot_nt(q_ref[rows[j], sl], kfull[keys[j]]) for j in range(nsub)]
        probs, dens = [], []
        for j in range(nsub):
            s = jnp.where(oks[j], scores[j] * scale - slopes[h] * dist, NEG_BIG)
            mx = jnp.max(s, axis=-1, keepdims=True)
            p = jnp.exp(s - mx)
            den = jnp.sum(p, axis=-1, keepdims=True)
            l_ref[h, out_rows(j), :] = jnp.broadcast_to(mx + jnp.log(den), (tq, dh))
            probs.append(p.astype(BF16))
            dens.append(den)
        for j in range(nsub):
            o_ref[h, out_rows(j), :] = _dot(probs[j], vfull[keys[j]]) / dens[j]


def _attention_group(qkv, g, batch, seq):
    dil = ATT_GROUPS[g][1]
    sub_len = seq // dil
    tqb = min(ATT_BLOCK, sub_len)
    nq = sub_len // tqb
    half = ATT_HALF
    w = ATT_OUT_WIDTH
    n = len(ATT_GROUPS) * ATT_HEADS
    slopes_all = np.power(np.float32(2.0), -8.0 * np.arange(1, n + 1, dtype=np.float32) / n)
    slopes = tuple(float(s) for s in slopes_all[g * ATT_HEADS:(g + 1) * ATT_HEADS])
    x = qkv.reshape(batch, sub_len, dil * 3 * w)
    per, last = tqb // half, sub_len // half - 1

    def spec(col, where):
        if where == 0:
            return pl.BlockSpec((None, tqb, w), lambda b, i, p: (b, i, p * 3 + col))
        if where < 0:
            return pl.BlockSpec((None, half, w), lambda b, i, p: (b, jnp.maximum(i * per - 1, 0), p * 3 + col))
        return pl.BlockSpec((None, half, w), lambda b, i, p: (b, jnp.minimum((i + 1) * per, last), p * 3 + col))

    nh, dh = ATT_HEADS, ATT_HEAD_DIM
    out_spec = pl.BlockSpec((nh, None, tqb * dil, dh), lambda b, i, p: (0, b, i, 0))
    out_shape = jax.ShapeDtypeStruct((nh, batch, seq, dh), F32)
    o, lse = pl.pallas_call(
        functools.partial(_attn_kernel, dil=dil, slopes=slopes, sub_len=sub_len),
        grid=(batch, nq, dil),
        in_specs=[spec(0, 0), spec(1, -1), spec(1, 0), spec(1, 1), spec(2, -1), spec(2, 0), spec(2, 1)],
        out_specs=[out_spec, out_spec],
        out_shape=[out_shape, out_shape],
        compiler_params=_params("parallel", "parallel", "arbitrary"),
        name=f"dilated_attn_g{g}",
    )(x, x, x, x, x, x, x)
    return o.reshape(nh, batch * seq, dh), lse.reshape(nh, batch * seq, dh)


def _att_merge_kernel(o0, o1, o2, l0, l1, l2, y_ref):
    dh = ATT_HEAD_DIM
    for h in range(ATT_HEADS):
        a, b, c = l0[h], l1[h], l2[h]
        mx = jnp.maximum(jnp.maximum(a, b), c)
        ea, eb, ec = jnp.exp(a - mx), jnp.exp(b - mx), jnp.exp(c - mx)
        y = (ea * o0[h] + eb * o1[h] + ec * o2[h]) / (ea + eb + ec)
        y_ref[:, h * dh:(h + 1) * dh] = y.astype(y_ref.dtype)


def _att_merge(outs, lses):
    nh, m, dh = outs[0].shape
    tm = ROW_TILE
    spec = pl.BlockSpec((nh, tm, dh), lambda i: (0, i, 0))
    return pl.pallas_call(
        _att_merge_kernel,
        grid=(m // tm,),
        in_specs=[spec] * 6,
        out_specs=pl.BlockSpec((tm, nh * dh), lambda i: (i, 0)),
        out_shape=jax.ShapeDtypeStruct((m, nh * dh), BF16),
        compiler_params=_params("parallel"),
        name="att_merge",
    )(*outs, *lses)


def _ret_kernel(lg_ref, qf_ref, kf_ref, vf_ref, qb_ref, kb_ref, vb_ref, of_ref, ob_ref, sf_ref, sb_ref):
    c, dk, dv = RET_CHUNK, RET_DK, RET_DV

    @pl.when(pl.program_id(1) == 0)
    def _():
        sf_ref[...] = jnp.zeros_like(sf_ref)
        sb_ref[...] = jnp.zeros_like(sb_ref)

    pos = lax.broadcasted_iota(jnp.int32, (c, 1), 0).astype(F32)
    diff = (lax.broadcasted_iota(jnp.int32, (c, c), 0) - lax.broadcasted_iota(jnp.int32, (c, c), 1)).astype(F32)
    up = jnp.maximum(diff, 0.0)
    dn = jnp.maximum(-diff, 0.0)
    heads = range(RET_HEADS)
    ks = [slice(h * dk, (h + 1) * dk) for h in heads]
    vs = [slice(h * dv, (h + 1) * dv) for h in heads]
    lgf = [jnp.full((1, 1), lg_ref[0, h], F32) for h in heads]
    lgb = [jnp.full((1, 1), lg_ref[1, h], F32) for h in heads]
    qf = [qf_ref[:, ks[h]].astype(BF16) for h in heads]
    kf = [kf_ref[:, ks[h]] * (dk ** -0.5) for h in heads]
    vf = [vf_ref[:, vs[h]].astype(BF16) for h in heads]
    qb = [qb_ref[:, ks[h]].astype(BF16) for h in heads]
    kb = [kb_ref[:, ks[h]] * (dk ** -0.5) for h in heads]
    vb = [vb_ref[:, vs[h]].astype(BF16) for h in heads]
    st_f = [sf_ref[h] for h in heads]
    st_b = [sb_ref[h] for h in heads]
    raw = [_dot_nt(qf[h], kf[h].astype(BF16)) for h in heads]
    cross_f = [_dot(qf[h], st_f[h].astype(BF16)) for h in heads]
    cross_b = [_dot(qb[h], st_b[h].astype(BF16)) for h in heads]
    scores = []
    for h in heads:
        dmat = jnp.where(diff >= 0, jnp.exp(up * lgf[h]), 0.0) + jnp.where(diff <= 0, jnp.exp(dn * lgb[h]), 0.0)
        scores.append((raw[h] * dmat).astype(BF16))
    for h in heads:
        of_ref[:, vs[h]] = _dot(scores[h], vf[h]) + cross_f[h] * jnp.exp((pos + 1.0) * lgf[h])
        ob_ref[:, vs[h]] = cross_b[h] * jnp.exp((c - pos) * lgb[h])
    for h in heads:
        kz = (kf[h] * jnp.exp((c - 1.0 - pos) * lgf[h])).astype(BF16)
        sf_ref[h] = jnp.exp(c * lgf[h]) * st_f[h] + _dot_tn(kz, vf[h])
        kz = (kb[h] * jnp.exp(pos * lgb[h])).astype(BF16)
        sb_ref[h] = jnp.exp(c * lgb[h]) * st_b[h] + _dot_tn(kz, vb[h])


def _retention(rproj, log_g, batch, seq):
    c = RET_CHUNK
    nc = seq // c
    qk_w, v_w = RET_HEADS * RET_DK, RET_HEADS * RET_DV
    x = rproj.reshape(batch, seq, rproj.shape[1])

    def tok(width, col, rev):
        return pl.BlockSpec((None, c, width), lambda b, ci: (b, nc - 1 - ci if rev else ci, col))

    shape = jax.ShapeDtypeStruct((batch, seq, v_w), F32)
    state = pltpu.VMEM((RET_HEADS, RET_DK, RET_DV), F32)
    v_col = 2 * qk_w // v_w
    return pl.pallas_call(
        _ret_kernel,
        grid=(batch, nc),
        in_specs=[pl.BlockSpec(memory_space=pltpu.SMEM),
                  tok(qk_w, 0, False), tok(qk_w, 1, False), tok(v_w, v_col, False),
                  tok(qk_w, 0, True), tok(qk_w, 1, True), tok(v_w, v_col, True)],
        out_specs=[tok(v_w, 0, False), tok(v_w, 0, True)],
        out_shape=[shape, shape],
        scratch_shapes=[state, state],
        compiler_params=_params("parallel", "arbitrary"),
        name="retention",
    )(log_g, x, x, x, x, x, x)


def _ret_post_kernel(rf_ref, rb_ref, g_ref, gain_ref, y_ref):
    ret = rf_ref[...] + rb_ref[...]
    gate = g_ref[...]
    gate = gate * _sigmoid(gate)
    gain = gain_ref[...]
    for h in range(RET_HEADS):
        sl = slice(h * RET_DV, (h + 1) * RET_DV)
        y = ret[:, sl]
        mu = jnp.mean(y, axis=-1, keepdims=True)
        yc = y - mu
        var = jnp.mean(yc * yc, axis=-1, keepdims=True)
        y_ref[:, sl] = (gate[:, sl] * yc * lax.rsqrt(var + RET_NORM_EPS) * gain[:, sl]).astype(y_ref.dtype)


def _ret_post(ret_f, ret_b, rproj, gain):
    batch, seq, w = ret_f.shape
    m = batch * seq
    tm = ROW_TILE
    gcol = rproj.shape[1] // w - 1
    row = pl.BlockSpec((tm, w), lambda i: (i, 0))
    return pl.pallas_call(
        _ret_post_kernel,
        grid=(m // tm,),
        in_specs=[row, row, pl.BlockSpec((tm, w), lambda i: (i, gcol)), pl.BlockSpec((1, w), lambda i: (0, 0))],
        out_specs=row,
        out_shape=jax.ShapeDtypeStruct((m, w), BF16),
        compiler_params=_params("parallel"),
        name="ret_post",
    )(ret_f.reshape(m, w), ret_b.reshape(m, w), rproj, gain.reshape(1, w))


def _rwkv_prep_kernel(u_ref, up_ref, un_ref, wf_ref, conv_ref, w0_ref, w2_ref, a0_ref, a2_ref, g2_ref,
                      rkv_ref, lwa_ref, g_ref, *, ns):
    i = pl.program_id(1)
    ts, halo = u_ref.shape[0], up_ref.shape[0]
    xa = _dot(jnp.concatenate([up_ref[...], u_ref[...], un_ref[...]], axis=0), wf_ref[...])
    x = xa[halo:halo + ts]
    w = RWKV_WIDTH
    row = lax.broadcasted_iota(jnp.int32, (ts, 1), 0)
    prev_row = xa[halo - 1:halo, :] * jnp.where(i > 0, 1.0, 0.0)
    next_row = xa[halo + ts:halo + ts + 1, :] * jnp.where(i < ns - 1, 1.0, 0.0)
    x_m1 = jnp.where(row == 0, prev_row, pltpu.roll(x, 1, 0))
    x_p1 = jnp.where(row == ts - 1, next_row, pltpu.roll(x, ts - 1, 0))
    cw = conv_ref[...]
    feats = cw[0:1, :] * x_m1 + cw[1:2, :] * x + cw[2:3, :] * x_p1
    rkv_ref[...] = feats[:, :3 * w]
    w_lo = jnp.tanh(feats[:, 3 * w:3 * w + 128]).astype(BF16)
    a_lo = feats[:, 3 * w + 128:3 * w + 256].astype(BF16)
    g_lo = _sigmoid(feats[:, 3 * w + 256:3 * w + 384]).astype(BF16)
    logw = -_softplus(-(w0_ref[...] + _dot(w_lo, w2_ref[...]))) - 0.5
    lwa_ref[:, :2 * w] = -jnp.exp(logw)
    lwa_ref[:, 2 * w:] = _sigmoid(a0_ref[...] + _dot(a_lo, a2_ref[...]))
    g_ref[...] = _dot(g_lo, g2_ref[...])


def _rwkv_prep(u, w_feat, feat_block, conv, w0, w2bd, a0, a2bd, g2, batch, seq):
    ts = 512
    halo = 16
    ns = seq // ts
    d, ch = w_feat.shape[0], RWKV_CONV_CH
    w = RWKV_WIDTH
    x = u.reshape(batch, seq, d)
    full = lambda shape: pl.BlockSpec(shape, lambda b, i: (0, 0))
    return pl.pallas_call(
        functools.partial(_rwkv_prep_kernel, ns=ns),
        grid=(batch, ns),
        in_specs=[
            pl.BlockSpec((None, ts, d), lambda b, i: (b, i, 0)),
            pl.BlockSpec((None, halo, d), lambda b, i: (b, jnp.maximum(i * (ts // halo) - 1, 0), 0)),
            pl.BlockSpec((None, halo, d), lambda b, i: (b, jnp.minimum((i + 1) * (ts // halo), seq // halo - 1), 0)),
            pl.BlockSpec((d, ch), lambda b, i: (0, feat_block)),
            full((3, ch)), full((1, 2 * w)), full((128, 2 * w)), full((1, 2 * w)), full((128, 2 * w)), full((128, w)),
        ],
        out_specs=[
            pl.BlockSpec((None, ts, 3 * w), lambda b, i: (b, i, 0)),
            pl.BlockSpec((None, ts, 4 * w), lambda b, i: (b, i, 0)),
            pl.BlockSpec((None, ts, w), lambda b, i: (b, i, 0)),
        ],
        out_shape=[
            jax.ShapeDtypeStruct((batch, seq, 3 * w), F32),
            jax.ShapeDtypeStruct((batch, seq, 4 * w), F32),
            jax.ShapeDtypeStruct((batch, seq, w), F32),
        ],
        compiler_params=_params("parallel", "arbitrary"),
        name="rwkv_prep",
    )(x, x, x, w_feat, conv, w0, w2bd, a0, a2bd, g2)


def _rwkv_fold_decay(r, k, v, lw, a, kk_scale, k_a, ones_bd, reverse):
    c = RWKV_CHUNK
    sgn = -1 if reverse else 1
    d = (lax.broadcasted_iota(jnp.int32, (c, c), 0) - lax.broadcasted_iota(jnp.int32, (c, c), 1)) * sgn
    tri = jnp.where(d >= 0, 1.0, 0.0).astype(BF16)
    hi = lw.astype(BF16)
    rem = lw - hi.astype(F32)
    mid = rem.astype(BF16)
    lo = (rem - mid.astype(F32)).astype(BF16)
    cs = _dot(tri, hi) + _dot(tri, mid) + _dot(tri, lo)
    g_all = jnp.exp(jnp.sum(lw, axis=0, keepdims=True))
    e_neg = jnp.exp(-cs)
    kks = k * kk_scale
    sq_hi, sq_lo = _split2(kks * kks)
    kk = kks * lax.rsqrt(_dot(sq_hi, ones_bd) + _dot(sq_lo, ones_bd) + 1e-12)
    abar_all = -kk * jnp.exp(cs - lw)
    bbar_all = kk * a * e_neg
    kbar_all = k * (1.0 + (a - 1.0) * k_a) * e_neg
    rbar_all = r * jnp.exp(cs)
    return abar_all, rbar_all, bbar_all, kbar_all, v, g_all


def _rwkv_masks(reverse):
    n, gw = RWKV_N, RWKV_GROUP_HEADS * RWKV_N
    row = lax.broadcasted_iota(jnp.int32, (gw, gw), 0)
    col = lax.broadcasted_iota(jnp.int32, (gw, gw), 1)
    same_head = (row // n) == (col // n)
    dd = jnp.where(same_head, (row - col) * (-1 if reverse else 1), -1)
    return same_head, dd > 0, dd >= 0, jnp.where(row == col, 1.0, 0.0)


def _rwkv_group_chain(folded, masks, g, s_ref):
    c, n, gh = RWKV_CHUNK, RWKV_N, RWKV_GROUP_HEADS
    gw = gh * n
    abar_all, rbar_all, bbar_all, kbar_all, v, g_all = folded
    same_head, strict, incl, eye = masks
    tile = lambda x: jnp.concatenate([x] * gh, axis=0)
    n_sq = c.bit_length() - 1
    gl = slice(g * gw, (g + 1) * gw)
    a_s = jnp.where(same_head, tile(abar_all[:, gl]), 0.0)
    r_s = jnp.where(same_head, tile(rbar_all[:, gl]), 0.0)
    v_s = jnp.where(same_head, tile(v[:, gl]), 0.0).astype(BF16)
    b_t = tile(bbar_all[:, gl].astype(BF16))
    k_t = tile(kbar_all[:, gl].astype(BF16))
    bk = jnp.concatenate([b_t, k_t], axis=0)
    f = _dot_nt(jnp.concatenate([a_s, r_s], axis=0).astype(BF16), bk)
    yield
    l_b = jnp.where(strict, f[:gw, :gw], 0.0)
    l_k = jnp.where(strict, f[:gw, gw:], 0.0).astype(BF16)
    m_b = jnp.where(incl, f[gw:, :gw], 0.0).astype(BF16)
    m_k = jnp.where(incl, f[gw:, gw:], 0.0).astype(BF16)
    tt = eye + l_b
    pwb = l_b.astype(BF16)
    pwb = _dot(pwb, pwb).astype(BF16)
    yield
    lkv = _dot(l_k, v_s).astype(BF16)
    yield
    for _ in range(n_sq - 2):
        both = _dot(pwb, jnp.concatenate([pwb, tt.astype(BF16)], axis=1))
        yield
        tt = tt + both[:, gw:]
        pwb = both[:, :gw].astype(BF16)
    tt = tt + _dot(pwb, tt.astype(BF16))
    yield
    zzb = _dot(tt.astype(BF16), jnp.concatenate([a_s.astype(BF16), lkv], axis=1)).astype(BF16)
    yield
    w_b = zzb[:, :gw]
    uv = jnp.concatenate([zzb[:, gw:], v_s], axis=0)
    qt = r_s + _dot(m_b, w_b)
    yield
    yloc = _dot(jnp.concatenate([m_b, m_k], axis=1), uv)
    yield
    state = s_ref[g]
    s_b = state.astype(BF16)
    y_s = _dot_nt(qt.astype(BF16), s_b) + yloc
    yield
    y = y_s[0:c]
    for i in range(1, gh):
        y = y + y_s[i * c:(i + 1) * c]
    gdec = g_all[:, gl]
    zb = _dot_tn(zzb, b_t)
    yield
    gmat = jnp.where(same_head, zb[gw:] + _dot_tn(v_s, k_t), 0.0) * gdec
    yield
    p_off = jnp.where(same_head, zb[:gw], 0.0) * gdec
    s_ref[g] = state * gdec + _dot(s_b, p_off.astype(BF16)) + gmat
    return y


def _rwkv_chunk_kernel(rf_ref, kf_ref, vf_ref, lwf_ref, af_ref, rb_ref, kb_ref, vb_ref, lwb_ref, ab_ref,
                       kk_ref, ka_ref, ones_ref, of_ref, ob_ref, sf_ref, sb_ref):
    @pl.when(pl.program_id(1) == 0)
    def _():
        sf_ref[...] = jnp.zeros_like(sf_ref)
        sb_ref[...] = jnp.zeros_like(sb_ref)

    kk_scale, k_a, ones_bd = kk_ref[...], ka_ref[...], ones_ref[...]
    masks = (_rwkv_masks(False), _rwkv_masks(True))
    ngroups = RWKV_HEADS // RWKV_GROUP_HEADS
    chains = []
    for b in range(rf_ref.shape[0]):
        fwd = _rwkv_fold_decay(rf_ref[b], kf_ref[b], vf_ref[b], lwf_ref[b], af_ref[b], kk_scale, k_a, ones_bd, False)
        bwd = _rwkv_fold_decay(rb_ref[b], kb_ref[b], vb_ref[b], lwb_ref[b], ab_ref[b], kk_scale, k_a, ones_bd, True)
        for g in range(ngroups):
            chains.append((of_ref, b, g, _rwkv_group_chain(fwd, masks[0], g, sf_ref.at[b])))
            chains.append((ob_ref, b, g, _rwkv_group_chain(bwd, masks[1], g, sb_ref.at[b])))
    gw = RWKV_GROUP_HEADS * RWKV_N
    while chains:
        live = []
        for o_ref, b, g, chain in chains:
            try:
                next(chain)
                live.append((o_ref, b, g, chain))
            except StopIteration as done:
                o_ref[b, :, g * gw:(g + 1) * gw] = done.value
        chains = live


def _rwkv_scan(rkv, lwa, kk_scale, k_a, batch, seq):
    c = RWKV_CHUNK
    nc = seq // c
    w = RWKV_WIDTH
    gw = RWKV_GROUP_HEADS * RWKV_N
    nb = math.gcd(batch, RWKV_BATCH_ROWS)
    head = np.arange(w) // RWKV_N
    ones_bd = jnp.asarray(head[:, None] == head[None, :], BF16)

    def tok(col, rev):
        return pl.BlockSpec((nb, c, w), lambda b, ci: (b, nc - 1 - ci if rev else ci, col))

    vec = pl.BlockSpec((1, w), lambda b, ci: (0, 0))
    shape = jax.ShapeDtypeStruct((batch, seq, w), F32)
    state = pltpu.VMEM((nb, RWKV_HEADS // RWKV_GROUP_HEADS, gw, gw), F32)
    return pl.pallas_call(
        _rwkv_chunk_kernel,
        grid=(batch // nb, nc),
        in_specs=[
            tok(0, False), tok(1, False), tok(2, False), tok(0, False), tok(2, False),
            tok(0, True), tok(1, True), tok(2, True), tok(1, True), tok(3, True),
            vec, vec, pl.BlockSpec((w, w), lambda b, ci: (0, 0)),
        ],
        out_specs=[tok(0, False), tok(0, True)],
        out_shape=[shape, shape],
        scratch_shapes=[state, state],
        compiler_params=_params("parallel", "arbitrary"),
        name="rwkv_scan",
    )(rkv, rkv, rkv, lwa, lwa, rkv, rkv, rkv, lwa, lwa, kk_scale, k_a, ones_bd)


def _rwkv_post_kernel(yf_ref, yb_ref, r_ref, k_ref, v_ref, a0_ref, a1_ref, g_ref, ka_ref, rk_ref, lnw_ref, lnb_ref, o_ref):
    n = RWKV_N
    y = yf_ref[...] + yb_ref[...]
    k, v, ka = k_ref[...], v_ref[...], ka_ref[...]
    kd = k * (1.0 + (a0_ref[...] - 1.0) * ka) + k * (1.0 + (a1_ref[...] - 1.0) * ka)
    t = r_ref[...] * kd * rk_ref[...]
    lnw, lnb = lnw_ref[...], lnb_ref[...]
    outs = []
    for h in range(RWKV_HEADS):
        sl = slice(h * n, (h + 1) * n)
        yh = y[:, sl]
        mu = jnp.mean(yh, axis=-1, keepdims=True)
        yc = yh - mu
        var = jnp.mean(yc * yc, axis=-1, keepdims=True)
        bonus = jnp.sum(t[:, sl], axis=-1, keepdims=True)
        outs.append(yc * lax.rsqrt(var + RWKV_NORM_EPS) * lnw[:, sl] + lnb[:, sl] + bonus * v[:, sl])
    o_ref[...] = (jnp.concatenate(outs, axis=1) * g_ref[...]).astype(o_ref.dtype)


def _rwkv_post(yf, yb, rkv, lwa, g, k_a, r_k, ln_w, ln_b):
    batch, seq, w = yf.shape
    m = batch * seq
    tm = ROW_TILE
    rkv2 = rkv.reshape(m, 3 * w)
    lwa2 = lwa.reshape(m, 4 * w)
    tok = lambda col: pl.BlockSpec((tm, w), lambda i: (i, col))
    vec = pl.BlockSpec((1, w), lambda i: (0, 0))
    return pl.pallas_call(
        _rwkv_post_kernel,
        grid=(m // tm,),
        in_specs=[tok(0), tok(0), tok(0), tok(1), tok(2), tok(2), tok(3), tok(0), vec, vec, vec, vec],
        out_specs=pl.BlockSpec((tm, w), lambda i: (i, 0)),
        out_shape=jax.ShapeDtypeStruct((m, w), BF16),
        compiler_params=_params("parallel"),
        name="rwkv_post",
    )(yf.reshape(m, w), yb.reshape(m, w), rkv2, rkv2, rkv2, lwa2, lwa2, g.reshape(m, w), k_a, r_k, ln_w, ln_b)


def _merge_kernel(u_ref, ya_ref, yb_ref, yc_ref, ga_ref, gb_ref, gc_ref, wa_ref, wb_ref, wc_ref, o_ref):
    u = u_ref[...]
    acc = _sigmoid(_dot(u, ga_ref[...])) * _dot(ya_ref[...], wa_ref[...])
    acc += _sigmoid(_dot(u, gb_ref[...])) * _dot(yb_ref[...], wb_ref[...])
    acc += _sigmoid(_dot(u, gc_ref[...])) * _dot(yc_ref[...], wc_ref[...])
    o_ref[...] = acc.astype(o_ref.dtype)


def _merge(u, ya, yb, yc, w_gates, wa, wb, wc):
    m, d = u.shape
    tm, tn = MM_ROW_TILE, 512
    nj = d // tn
    act = lambda a: pl.BlockSpec((tm, a.shape[1]), lambda j, i: (i, 0))
    wsp = lambda a: pl.BlockSpec((a.shape[0], tn), lambda j, i: (0, j))
    gsp = lambda b: pl.BlockSpec((d, tn), lambda j, i: (0, b * nj + j))
    return pl.pallas_call(
        _merge_kernel,
        grid=(nj, m // tm),
        in_specs=[act(u), act(ya), act(yb), act(yc), gsp(0), gsp(1), gsp(2), wsp(wa), wsp(wb), wsp(wc)],
        out_specs=pl.BlockSpec((tm, tn), lambda j, i: (i, j)),
        out_shape=jax.ShapeDtypeStruct((m, d), BF16),
        compiler_params=_params("parallel", "arbitrary"),
        name="branch_merge",
    )(u, ya, yb, yc, w_gates, w_gates, w_gates, wa, wb, wc)


def _out_proj_kernel(x_ref, m_ref, w_ref, o_ref):
    o_ref[...] = x_ref[...] + _dot(m_ref[...], w_ref[...])


def _out_proj(x, merged, w):
    m, d = x.shape
    tm = ROW_TILE
    row = pl.BlockSpec((tm, d), lambda i: (i, 0))
    return pl.pallas_call(
        _out_proj_kernel,
        grid=(m // tm,),
        in_specs=[row, row, pl.BlockSpec((d, d), lambda i: (0, 0))],
        out_specs=row,
        out_shape=jax.ShapeDtypeStruct((m, d), F32),
        compiler_params=_params("parallel"),
        name="out_proj",
    )(x, merged, w)


def _block_diag2(w2):
    zero = jnp.zeros_like(w2[0])
    return jnp.concatenate([jnp.concatenate([w2[0], zero], axis=1), jnp.concatenate([zero, w2[1]], axis=1)], axis=0)


def _mixers(x, u, p, batch, seq):
    ng, aw = len(ATT_GROUPS), ATT_OUT_WIDTH
    att3 = 3 * ng * aw
    ret_w = 2 * RET_HEADS * RET_DK + 2 * RET_HEADS * RET_DV
    outs, lses = [], []
    for g in range(ng):
        qkv = _matmul(u, p["w_qkv"][g], 3 * aw, 768, BF16, 0, 1)
        o, l = _attention_group(qkv, g, batch, seq)
        outs.append(o)
        lses.append(l)
    ya = _att_merge(outs, lses)

    rproj = _matmul(u, p["w_ret"], ret_w, 1024, F32, 0, 1)
    ret_f, ret_b = _retention(rproj, p["ret_log_g"], batch, seq)
    yb = _ret_post(ret_f, ret_b, rproj, p["ret_norm"])

    rkv, lwa, g = _rwkv_prep(u, p["w_in"], (att3 + ret_w) // RWKV_CONV_CH, p["rwkv_conv"], p["rwkv_w0"], p["rwkv_w2bd"], p["rwkv_a0"], p["rwkv_a2bd"],
                             p["rwkv_g2"], batch, seq)
    yf, yb_dir = _rwkv_scan(rkv, lwa, p["rwkv_k_k"], p["rwkv_k_a"], batch, seq)
    yc = _rwkv_post(yf, yb_dir, rkv, lwa, g, p["rwkv_k_a"], p["rwkv_r_k"], p["rwkv_ln_w"], p["rwkv_ln_b"])

    merged = _merge(u, ya, yb, yc, p["w_gates"], p["w_branch_a"], p["w_branch_b"], p["w_branch_c"])
    return _out_proj(x, merged, p["w_out"])


def kernel(x_prompt, x_sample, ffn1_norm, ffn1_w_gate, ffn1_w_up, ffn1_w_down, mix_norm, w_in, ret_decay_logit, ret_norm, rwkv_conv, rwkv_w0, rwkv_w2, rwkv_a0, rwkv_a2, rwkv_g2, rwkv_k_k, rwkv_k_a, rwkv_r_k, rwkv_ln_w, rwkv_ln_b, w_branch_a, w_branch_b, w_branch_c, w_out, ffn2_norm, ffn2_w_gate, ffn2_w_up, ffn2_w_down, final_norm):
    depth = ffn1_norm.shape[0]
    w = RWKV_WIDTH
    gates0 = w_in.shape[2] - 3 * x_prompt.shape[2]
    ng, aw = len(ATT_GROUPS), ATT_OUT_WIDTH
    ret_w = 2 * RET_HEADS * RET_DK + 2 * RET_HEADS * RET_DV
    layers = []
    for l in range(depth):
        w_in_l = w_in[l].astype(BF16)
        layers.append({
            "ffn1": (ffn1_norm[l], ffn1_w_gate[l].astype(BF16), ffn1_w_up[l].astype(BF16), ffn1_w_down[l].astype(BF16)),
            "ffn2": (ffn2_norm[l], ffn2_w_gate[l].astype(BF16), ffn2_w_up[l].astype(BF16), ffn2_w_down[l].astype(BF16)),
            "mix_norm": mix_norm[l],
            "w_in": w_in_l, "w_gates": w_in_l[:, gates0:],
            "w_qkv": [jnp.concatenate([w_in_l[:, (t * ng + g) * aw:(t * ng + g + 1) * aw] for t in range(3)], axis=1)
                      for g in range(ng)],
            "w_ret": w_in_l[:, 3 * ng * aw:3 * ng * aw + ret_w],
            "ret_log_g": jax.nn.log_sigmoid(ret_decay_logit[l].astype(F32)),
            "ret_norm": ret_norm[l].astype(F32),
            "rwkv_conv": rwkv_conv[l].astype(F32),
            "rwkv_w0": rwkv_w0[l].astype(F32).reshape(1, 2 * w),
            "rwkv_w2bd": _block_diag2(rwkv_w2[l]).astype(BF16),
            "rwkv_a0": rwkv_a0[l].astype(F32).reshape(1, 2 * w),
            "rwkv_a2bd": _block_diag2(rwkv_a2[l]).astype(BF16),
            "rwkv_g2": rwkv_g2[l].astype(BF16),
            "rwkv_k_k": rwkv_k_k[l].astype(F32).reshape(1, w),
            "rwkv_k_a": rwkv_k_a[l].astype(F32).reshape(1, w),
            "rwkv_r_k": rwkv_r_k[l].astype(F32).reshape(1, w),
            "rwkv_ln_w": rwkv_ln_w[l].astype(F32).reshape(1, w),
            "rwkv_ln_b": rwkv_ln_b[l].astype(F32).reshape(1, w),
            "w_branch_a": w_branch_a[l].astype(BF16), "w_branch_b": w_branch_b[l].astype(BF16),
            "w_branch_c": w_branch_c[l].astype(BF16), "w_out": w_out[l].astype(BF16),
        })

    def run(x):
        batch, seq, d = x.shape
        y = x.reshape(batch * seq, d)
        for l, p in enumerate(layers):
            y, u = _ffn(y, *p["ffn1"], p["mix_norm"], "next_norm")
            y = _mixers(y, u, p, batch, seq)
            y, = _ffn(y, *p["ffn2"], final_norm, "final" if l == depth - 1 else "plain")
        return y.reshape(batch, seq, d)

    return (run(x_prompt), run(x_sample))
```
